```python
import math
import jax, jax.numpy as jnp
from jax import lax
import numpy as np

D_MODEL = 2048
BATCH = 4
SEQ = 4096
DEPTH = 1
DEC_BATCH = 32
DEC_SEQ = 8
PAST_LEN = 16384
PAGE_SIZE = 128

N_META = 16
N_HEADS = 8
HEAD_DIM = 128
N_KV_HEADS = 2
ATTN_WIDTH = N_HEADS * HEAD_DIM
CONV_CH = D_MODEL - ATTN_WIDTH
CONV_W = 31
IDX_HEADS = 16
IDX_DIM = 64
IDX_SCALE = (IDX_HEADS * IDX_DIM) ** -0.5
TOPK_MAX = 256
N_BUCKETS = 32
MAX_DISTANCE = 128
ATTN_SCALE = HEAD_DIM ** -0.5
Q_BLOCK = 128
PEER_HEADS = 8
PEER_NKEYS = 128
PEER_EXPERTS = PEER_NKEYS * PEER_NKEYS
PEER_KDIM = 128
PEER_TOPK = 16
PEER_BLOCK = 128
EPS = 1e-6
IN_COLS = (ATTN_WIDTH, N_KV_HEADS * HEAD_DIM, N_KV_HEADS * HEAD_DIM,
           IDX_HEADS * IDX_DIM, IDX_DIM, IDX_HEADS, 2 * CONV_CH)
D_IN = sum(IN_COLS)

kernel_name = "hymba_dsa_conformer_peer_step"


def rmsnorm(x, g):
    xf = x.astype(jnp.float32)
    y = xf * lax.rsqrt(jnp.mean(xf * xf, axis=-1, keepdims=True) + EPS)
    return (y * g.astype(jnp.float32)).astype(x.dtype)


def t5_bucket(dist):
    n = jnp.maximum(dist, 0)
    max_exact = N_BUCKETS // 2
    nf = jnp.maximum(n, 1).astype(jnp.float32)
    large = max_exact + (jnp.log(nf / max_exact) / math.log(MAX_DISTANCE / max_exact)
                         * (N_BUCKETS - max_exact)).astype(jnp.int32)
    large = jnp.minimum(large, N_BUCKETS - 1)
    return jnp.where(n < max_exact, n, large)


def project_inputs(hn, w_in, g_q, g_k):
    B, T, _ = hn.shape
    offs = [sum(IN_COLS[:i + 1]) for i in range(len(IN_COLS) - 1)]
    q, k, v, qi, ki, wi, u = jnp.split(hn @ w_in, offs, axis=-1)
    q = rmsnorm(q.reshape(B, T, N_HEADS, HEAD_DIM), g_q)
    k = rmsnorm(k.reshape(B, T, N_KV_HEADS, HEAD_DIM), g_k)
    v = v.reshape(B, T, N_KV_HEADS, HEAD_DIM)
    qi = qi.reshape(B, T, IDX_HEADS, IDX_DIM)
    return q, k, v, qi, ki, wi * IDX_SCALE, u


def index_select(qi, wi, ki, qpos, topk):
    s = jax.nn.relu(jnp.einsum('bqhd,bld->bqhl', qi, ki).astype(jnp.float32))
    score = jnp.einsum('bqh,bqhl->bql', wi.astype(jnp.float32), s)
    L = ki.shape[1]
    admissible = jnp.arange(L, dtype=jnp.int32)[None, None, :] <= qpos[None, :, None]
    score = jnp.where(admissible, score, -jnp.inf)
    _, idx = lax.top_k(score, topk)
    return idx


def take_rows(a, idx):
    return jax.vmap(lambda ab, ib: ab[ib])(a, idx)


def gather_paged(pool, new, page_table, idx):
    n_pages = page_table.shape[1]
    page = jnp.clip(idx // PAGE_SIZE, 0, n_pages - 1)
    phys = jax.vmap(lambda pt, pg: pt[pg])(page_table, page)
    past = pool[phys, idx % PAGE_SIZE]
    cur = take_rows(new, jnp.clip(idx - PAST_LEN, 0, new.shape[1] - 1))
    in_past = (idx < PAST_LEN).reshape(idx.shape + (1,) * (past.ndim - idx.ndim))
    return jnp.where(in_past, past, cur)


def sparse_attend(q, kg, vg, idx, qpos, rel_bias):
    B, Q = q.shape[:2]
    K = idx.shape[-1]
    G = N_HEADS // N_KV_HEADS
    qg = q.reshape(B, Q, N_KV_HEADS, G, HEAD_DIM)
    logits = jnp.einsum('bqngd,bqknd->bqngk', qg, kg).astype(jnp.float32) * ATTN_SCALE
    dist = qpos[None, :, None] - idx
    bias = rel_bias.astype(jnp.float32)[t5_bucket(dist)]
    bias = bias.reshape(B, Q, K, N_KV_HEADS, G).transpose(0, 1, 3, 4, 2)
    logits = jnp.where((dist >= 0)[:, :, None, None, :], logits + bias, -jnp.inf)
    p = jax.nn.softmax(logits, axis=-1).astype(vg.dtype)
    return jnp.einsum('bqngk,bqknd->bqngd', p, vg).reshape(B, Q, ATTN_WIDTH)


def conv_module(u, prev, dw_w, dw_b, ln_g, ln_b):
    a = u[..., :CONV_CH] * jax.nn.sigmoid(u[..., CONV_CH:])
    xp = jnp.concatenate([prev.astype(a.dtype), a], axis=1)
    y = lax.conv_general_dilated(xp, dw_w[:, None, :].astype(a.dtype), (1,), 'VALID',
                                 dimension_numbers=('NWC', 'WIO', 'NWC'),
                                 feature_group_count=CONV_CH) + dw_b
    yf = y.astype(jnp.float32)
    mu = jnp.mean(yf, axis=-1, keepdims=True)
    var = jnp.mean(jnp.square(yf - mu), axis=-1, keepdims=True)
    yn = (yf - mu) * lax.rsqrt(var + EPS) * ln_g.astype(jnp.float32) + ln_b.astype(jnp.float32)
    return jax.nn.silu(yn).astype(u.dtype), xp[:, -(CONV_W - 1):]


def peer_route(x2d, w_pq, sub_keys):
    N = x2d.shape[0]
    q = (x2d @ w_pq).reshape(N, PEER_HEADS, 2, PEER_KDIM // 2)
    s = jnp.einsum('nhcd,ckd->nhck', q, sub_keys).astype(jnp.float32)
    v1, i1 = lax.top_k(s[:, :, 0], PEER_TOPK)
    v2, i2 = lax.top_k(s[:, :, 1], PEER_TOPK)
    cand = (v1[..., :, None] + v2[..., None, :]).reshape(N, PEER_HEADS, PEER_TOPK * PEER_TOPK)
    cidx = (i1[..., :, None] * PEER_NKEYS + i2[..., None, :]).reshape(N, PEER_HEADS, PEER_TOPK * PEER_TOPK)
    vals, pos = lax.top_k(cand, PEER_TOPK)
    experts = jnp.take_along_axis(cidx, pos, axis=-1)
    return experts, jax.nn.softmax(vals, axis=-1)


def peer_ffn(x2d, w_pq, sub_keys, peer_u, peer_v):
    N = x2d.shape[0]
    experts, gates = peer_route(x2d, w_pq, sub_keys)
    nb = -(-N // PEER_BLOCK)
    pad = nb * PEER_BLOCK - N
    xb = jnp.pad(x2d, ((0, pad), (0, 0))).reshape(nb, PEER_BLOCK, D_MODEL)
    eb = jnp.pad(experts, ((0, pad), (0, 0), (0, 0))).reshape(nb, PEER_BLOCK, PEER_HEADS, PEER_TOPK)
    gb = jnp.pad(gates, ((0, pad), (0, 0), (0, 0))).reshape(nb, PEER_BLOCK, PEER_HEADS, PEER_TOPK)

    def block(args):
        xc, ec, gc = args
        h = jnp.einsum('nd,nhkd->nhk', xc, peer_u[ec])
        act = (gc * jax.nn.gelu(h.astype(jnp.float32))).astype(xc.dtype)
        return jnp.einsum('nhk,nhkd->nd', act, peer_v[ec])

    out = lax.map(block, (xb, eb, gb))
    return out.reshape(nb * PEER_BLOCK, D_MODEL)[:N]


def mix_tail(x, o, c, w_out, g_ffn, pwq, psk, pu, pv):
    B, T, _ = x.shape
    h = x + jnp.concatenate([o, c], axis=-1) @ w_out
    f = peer_ffn(rmsnorm(h, g_ffn).reshape(B * T, D_MODEL), pwq, psk, pu, pv)
    return h + f.reshape(B, T, D_MODEL)


def to_blocks(a, nb):
    B, T = a.shape[:2]
    a = jnp.pad(a, [(0, 0), (0, nb * Q_BLOCK - T)] + [(0, 0)] * (a.ndim - 2))
    return jnp.moveaxis(a.reshape((B, nb, Q_BLOCK) + a.shape[2:]), 1, 0)


def prompt_layer(x, lp, rel_bias):
    (g_attn, w_in, g_q, g_k, dw_w, dw_b, ln_g, ln_b, w_out, g_ffn, pwq, psk, pu, pv) = lp
    B, T, _ = x.shape
    q, k, v, qi, ki, wi, u = project_inputs(rmsnorm(x, g_attn), w_in, g_q, g_k)
    topk = min(TOPK_MAX, SEQ // 4)
    nb = -(-T // Q_BLOCK)

    def attn_block(args):
        bi, qb, qib, wib = args
        qpos = bi * Q_BLOCK + jnp.arange(Q_BLOCK, dtype=jnp.int32)
        idx = index_select(qib, wib, ki, qpos, topk)
        return sparse_attend(qb, take_rows(k, idx), take_rows(v, idx), idx, qpos, rel_bias)

    o = lax.map(attn_block, (jnp.arange(nb, dtype=jnp.int32), to_blocks(q, nb),
                             to_blocks(qi, nb), to_blocks(wi, nb)))
    o = jnp.moveaxis(o, 0, 1).reshape(B, nb * Q_BLOCK, ATTN_WIDTH)[:, :T]
    c, conv_state = conv_module(u, jnp.zeros((B, CONV_W - 1, CONV_CH), u.dtype), dw_w, dw_b, ln_g, ln_b)
    y = mix_tail(x, o, c, w_out, g_ffn, pwq, psk, pu, pv)
    return y, (k, v, ki, conv_state)


def sample_layer(x, ck, cv, cki, sconv, page_table, lp, rel_bias):
    (g_attn, w_in, g_q, g_k, dw_w, dw_b, ln_g, ln_b, w_out, g_ffn, pwq, psk, pu, pv) = lp
    B, S, _ = x.shape
    q, k, v, qi, ki, wi, u = project_inputs(rmsnorm(x, g_attn), w_in, g_q, g_k)
    qpos = PAST_LEN + jnp.arange(S, dtype=jnp.int32)
    topk = min(TOPK_MAX, (PAST_LEN + S) // 4)
    ki_past = cki[page_table].reshape(B, PAST_LEN, IDX_DIM)
    idx = index_select(qi, wi, jnp.concatenate([ki_past, ki.astype(ki_past.dtype)], axis=1), qpos, topk)
    kg = gather_paged(ck, k.astype(ck.dtype), page_table, idx)
    vg = gather_paged(cv, v.astype(cv.dtype), page_table, idx)
    o = sparse_attend(q, kg, vg, idx, qpos, rel_bias)
    c, conv_state = conv_module(u, sconv, dw_w, dw_b, ln_g, ln_b)
    y = mix_tail(x, o, c, w_out, g_ffn, pwq, psk, pu, pv)
    return y, (k, v, ki, conv_state)


def setup_inputs(seed: int = 0) -> dict:
    key = jax.random.key(seed)
    ks = jax.random.split(key, 24)
    n_pages = PAST_LEN // PAGE_SIZE
    n_used = DEC_BATCH * n_pages
    n_phys = n_used + n_used // 4
    f32 = jnp.float32
    nrm = lambda k, s, sc: jax.random.normal(k, s, f32) * sc
    perm = jax.random.permutation(ks[6], n_phys).astype(jnp.int32)
    return {
        'x_prompt': nrm(ks[0], (BATCH, SEQ, D_MODEL), 1.0),
        'x_sample': nrm(ks[1], (DEC_BATCH, DEC_SEQ, D_MODEL), 1.0),
        'cache_k': nrm(ks[2], (DEPTH, n_phys, PAGE_SIZE, N_KV_HEADS, HEAD_DIM), 1.0),
        'cache_v': nrm(ks[3], (DEPTH, n_phys, PAGE_SIZE, N_KV_HEADS, HEAD_DIM), 1.0),
        'cache_kidx': nrm(ks[4], (DEPTH, n_phys, PAGE_SIZE, IDX_DIM), 1.0),
        'state_conv': nrm(ks[5], (DEPTH, DEC_BATCH, CONV_W - 1, CONV_CH), 0.5),
        'page_table': perm[:n_used].reshape(DEC_BATCH, n_pages),
        'meta_tokens': nrm(ks[7], (N_META, D_MODEL), 1.0),
        'rel_bias': nrm(ks[8], (N_BUCKETS, N_HEADS), 0.5),
        'g_attn': 1.0 + nrm(ks[9], (DEPTH, D_MODEL), 0.05),
        'w_in': nrm(ks[10], (DEPTH, D_MODEL, D_IN), D_MODEL ** -0.5),
        'g_q': 1.0 + nrm(ks[11], (DEPTH, HEAD_DIM), 0.05),
        'g_k': 1.0 + nrm(ks[12], (DEPTH, HEAD_DIM), 0.05),
        'conv_dw_w': nrm(ks[13], (DEPTH, CONV_W, CONV_CH), CONV_W ** -0.5),
        'conv_dw_b': nrm(ks[14], (DEPTH, CONV_CH), 0.01),
        'conv_ln_g': 1.0 + nrm(ks[15], (DEPTH, CONV_CH), 0.05),
        'conv_ln_b': nrm(ks[16], (DEPTH, CONV_CH), 0.01),
        'w_out': nrm(ks[17], (DEPTH, D_MODEL, D_MODEL), D_MODEL ** -0.5),
        'g_ffn': 1.0 + nrm(ks[18], (DEPTH, D_MODEL), 0.05),
        'peer_wq': nrm(ks[19], (DEPTH, D_MODEL, PEER_HEADS * PEER_KDIM), D_MODEL ** -0.5),
        'peer_subkeys': nrm(ks[20], (DEPTH, 2, PEER_NKEYS, PEER_KDIM // 2), (PEER_KDIM // 2) ** -0.5),
        'peer_u': nrm(ks[21], (DEPTH, PEER_EXPERTS, D_MODEL), D_MODEL ** -0.5),
        'peer_v': nrm(ks[22], (DEPTH, PEER_EXPERTS, D_MODEL), PEER_HEADS ** -0.5),
    }


def reference(x_prompt, x_sample, cache_k, cache_v, cache_kidx, state_conv, page_table,
              meta_tokens, rel_bias, g_attn, w_in, g_q, g_k, conv_dw_w, conv_dw_b,
              conv_ln_g, conv_ln_b, w_out, g_ffn, peer_wq, peer_subkeys, peer_u, peer_v):
    B = x_prompt.shape[0]
    meta = jnp.broadcast_to(meta_tokens[None].astype(x_prompt.dtype), (B, N_META, D_MODEL))
    xp = jnp.concatenate([meta, x_prompt], axis=1)
    xs = x_sample
    sp, ss = [], []
    for l in range(DEPTH):
        lp = (g_attn[l], w_in[l], g_q[l], g_k[l], conv_dw_w[l], conv_dw_b[l], conv_ln_g[l],
              conv_ln_b[l], w_out[l], g_ffn[l], peer_wq[l], peer_subkeys[l], peer_u[l], peer_v[l])
        xp, st_p = prompt_layer(xp, lp, rel_bias)
        xs, st_s = sample_layer(xs, cache_k[l], cache_v[l], cache_kidx[l], state_conv[l],
                                page_table, lp, rel_bias)
        sp.append(st_p)
        ss.append(st_s)
    y_prompt = xp[:, N_META:]
    y_sample = xs
    new_k_prompt = jnp.stack([s[0] for s in sp])
    new_v_prompt = jnp.stack([s[1] for s in sp])
    new_kidx_prompt = jnp.stack([s[2] for s in sp])
    new_conv_prompt = jnp.stack([s[3] for s in sp])
    new_k_sample = jnp.stack([s[0] for s in ss])
    new_v_sample = jnp.stack([s[1] for s in ss])
    new_kidx_sample = jnp.stack([s[2] for s in ss])
    new_conv_sample = jnp.stack([s[3] for s in ss])
    return (y_prompt, y_sample, new_k_prompt, new_v_prompt, new_kidx_prompt, new_conv_prompt,
            new_k_sample, new_v_sample, new_kidx_sample, new_conv_sample)
```

```python
import functools
import math
from typing import NamedTuple

import jax
import jax.numpy as jnp
from jax import lax
from jax.experimental import pallas as pl
from jax.experimental.pallas import tpu as pltpu

F32 = jnp.float32
BF16 = jnp.bfloat16
I32 = jnp.int32

D_MODEL = 2048
N_META = 16
N_HEADS = 8
HEAD_DIM = 128
N_KV_HEADS = 2
GROUP = N_HEADS // N_KV_HEADS
ATTN_WIDTH = N_HEADS * HEAD_DIM
KV_WIDTH = N_KV_HEADS * HEAD_DIM
CONV_CH = D_MODEL - ATTN_WIDTH
CONV_W = 31
IDX_HEADS = 16
IDX_DIM = 64
IDX_SCALE = (IDX_HEADS * IDX_DIM) ** -0.5
TOPK_MAX = 256
N_BUCKETS = 32
MAX_DISTANCE = 128
ATTN_SCALE = HEAD_DIM ** -0.5
PAGE_SIZE = 128
PEER_HEADS = 8
PEER_NKEYS = 128
PEER_KDIM = 128
PEER_TOPK = 16
PEER_SLOTS = PEER_HEADS * PEER_TOPK
EPS = 1e-6

LANES = 128
SUBLANES = 8
QB = 128
ROW_TILE = 256
CONV_HALO = 32
PEER_TOK_BLOCK = 64
PEER_UNROLL = 8
INT_MIN = -(2 ** 31)
NEG_BIG = -1e30
VMEM_LIMIT = 56 * 1024 * 1024

NT_DIMS = (((1,), (1,)), ((), ()))
TN_DIMS = (((0,), (0,)), ((), ()))


class Dims(NamedTuple):
    batch: int
    seq: int
    dec_batch: int
    dec_seq: int
    past_len: int

    @property
    def t_prompt(self):
        return self.seq + N_META

    @property
    def nq(self):
        return -(-self.t_prompt // QB)

    @property
    def tp(self):
        return self.nq * QB

    @property
    def prompt_rows(self):
        return self.batch * self.tp

    @property
    def sample_rows(self):
        return self.dec_batch * self.dec_seq

    @property
    def rows(self):
        r = self.prompt_rows + self.sample_rows
        return -(-r // ROW_TILE) * ROW_TILE

    @property
    def n_pages(self):
        return self.past_len // PAGE_SIZE


def _dot(a, b):
    return jnp.dot(a, b, preferred_element_type=F32)


def _dot_nt(a, b):
    return lax.dot_general(a, b, NT_DIMS, preferred_element_type=F32)


def _dot_tn(a, b):
    return lax.dot_general(a, b, TN_DIMS, preferred_element_type=F32)


def _params(sem):
    return pltpu.CompilerParams(dimension_semantics=sem, vmem_limit_bytes=VMEM_LIMIT)


def _const_spec(shape):
    nd = len(shape)
    return pl.BlockSpec(shape, lambda *_: (0,) * nd)


def _weight_spec(shape):
    nd = len(shape)
    return pl.BlockSpec(shape, lambda *_: (0,) * nd, pipeline_mode=pl.Buffered(1))


def _order_key(x):
    bits = pltpu.bitcast(x + 0.0, I32)
    return bits ^ ((bits >> 31) & 0x7FFFFFFF)


def _inproj_kernel(x_ref, ga_ref, wq_ref, wkv_ref, wqi_ref, wkw_ref, wu_ref, gq_ref, gk_ref,
                   q_ref, kf_ref, vf_ref, kb_ref, vb_ref, qi_ref, kw_ref, kib_ref, a_ref):
    x = x_ref[...]
    ms = jnp.mean(x * x, axis=-1, keepdims=True)
    hn = (x * lax.rsqrt(ms + EPS) * ga_ref[...]).astype(BF16)

    def head_norm(z, g):
        zm = jnp.mean(z * z, axis=-1, keepdims=True)
        return z * lax.rsqrt(zm + EPS) * g

    q = _dot(hn, wq_ref[...])
    for h in range(N_HEADS):
        sl = slice(h * HEAD_DIM, (h + 1) * HEAD_DIM)
        q_ref[:, sl] = head_norm(q[:, sl], gq_ref[...]).astype(BF16)
    kv = _dot(hn, wkv_ref[...])
    for n in range(N_KV_HEADS):
        sl = slice(n * HEAD_DIM, (n + 1) * HEAD_DIM)
        kn = head_norm(kv[:, sl], gk_ref[...])
        kf_ref[:, sl] = kn
        kb_ref[:, sl] = kn.astype(BF16)
    v = kv[:, KV_WIDTH:]
    vf_ref[...] = v
    vb_ref[...] = v.astype(BF16)
    qi_ref[...] = _dot(hn, wqi_ref[...]).astype(BF16)
    kw = _dot(hn, wkw_ref[...])
    lane = lax.broadcasted_iota(I32, kw.shape, 1)
    kw = jnp.where(lane >= IDX_DIM, kw * IDX_SCALE, kw)
    kw_ref[...] = kw
    kib_ref[...] = kw[:, :IDX_DIM].astype(BF16)
    u = _dot(hn, wu_ref[...])
    a_ref[...] = u[:, :CONV_CH] * jax.nn.sigmoid(u[:, CONV_CH:])


def _in_projection(x_all, g_attn, w_in, g_q, g_k):
    rows = x_all.shape[0]
    o_q = ATTN_WIDTH
    o_k = o_q + KV_WIDTH
    o_v = o_k + KV_WIDTH
    o_qi = o_v + IDX_HEADS * IDX_DIM
    o_ki = o_qi + IDX_DIM
    o_wi = o_ki + IDX_HEADS
    wq = w_in[:, :o_q].astype(BF16)
    wkv = w_in[:, o_q:o_v].astype(BF16)
    wqi = w_in[:, o_v:o_qi].astype(BF16)
    wkw = jnp.pad(w_in[:, o_qi:o_wi], ((0, 0), (0, LANES - IDX_DIM - IDX_HEADS))).astype(BF16)
    wu = w_in[:, o_wi:].astype(BF16)
    row = lambda w: pl.BlockSpec((ROW_TILE, w), lambda i: (i, 0))
    outs = [
        (ATTN_WIDTH, BF16), (KV_WIDTH, F32), (KV_WIDTH, F32), (KV_WIDTH, BF16), (KV_WIDTH, BF16),
        (IDX_HEADS * IDX_DIM, BF16), (LANES, F32), (IDX_DIM, BF16), (CONV_CH, F32),
    ]
    return pl.pallas_call(
        _inproj_kernel,
        grid=(rows // ROW_TILE,),
        in_specs=[row(D_MODEL), _const_spec((1, D_MODEL)), _weight_spec(wq.shape), _weight_spec(wkv.shape),
                  _weight_spec(wqi.shape), _weight_spec(wkw.shape), _weight_spec(wu.shape),
                  _const_spec((1, HEAD_DIM)), _const_spec((1, HEAD_DIM))],
        out_specs=[row(w) for w, _ in outs],
        out_shape=[jax.ShapeDtypeStruct((rows, w), dt) for w, dt in outs],
        compiler_params=_params(("arbitrary",)),
        name="in_projection",
    )(x_all, g_attn.reshape(1, D_MODEL), wq, wkv, wqi, wkw, wu, g_q.reshape(1, HEAD_DIM), g_k.reshape(1, HEAD_DIM))


def _t5_bias_of(dist, rb_ref, h):
    n = jnp.maximum(dist, 0)
    max_exact = N_BUCKETS // 2
    nf = jnp.maximum(n, 1).astype(F32)
    large = max_exact + (jnp.log(nf / max_exact) / math.log(MAX_DISTANCE / max_exact)
                         * (N_BUCKETS - max_exact)).astype(I32)
    large = jnp.minimum(large, N_BUCKETS - 1)
    bucket = jnp.where(n < max_exact, n, large)
    out = jnp.zeros(dist.shape, F32)
    for k in range(N_BUCKETS):
        out = jnp.where(bucket == k, rb_ref[k, h], out)
    return out


def _bias_kernel(rb_ref, pb_ref, dn_ref, dc_ref):
    r2 = lax.broadcasted_iota(I32, (2 * QB, QB), 0)
    c2 = lax.broadcasted_iota(I32, (2 * QB, QB), 1)
    r1 = lax.broadcasted_iota(I32, (SUBLANES, LANES), 0)
    c1 = lax.broadcasted_iota(I32, (SUBLANES, LANES), 1)
    for h in range(N_HEADS):
        pb_ref[h] = _t5_bias_of(QB + c2 - r2, rb_ref, h)
        dn_ref[h] = _t5_bias_of(PAGE_SIZE + r1 - c1, rb_ref, h)
        dc_ref[h] = _t5_bias_of(r1 - c1, rb_ref, h)


def _bias_tiles(rel_bias):
    return pl.pallas_call(
        _bias_kernel,
        in_specs=[pl.BlockSpec(memory_space=pltpu.SMEM)],
        out_shape=[jax.ShapeDtypeStruct((N_HEADS, 2 * QB, QB), F32),
                   jax.ShapeDtypeStruct((N_HEADS, SUBLANES, LANES), F32),
                   jax.ShapeDtypeStruct((N_HEADS, SUBLANES, LANES), F32)],
        name="t5_bias_tiles",
    )(rel_bias)


def _prompt_attn_kernel(qi_ref, q_ref, kw_ref, ki_ref, k_ref, v_ref, pb_ref, cb_ref, o_ref,
                        key_sc, acc_sc, m_sc, l_sc, *, topk, lc):
    i = pl.program_id(1)
    q_lo = i * QB
    n_chunks = (q_lo + QB + lc - 1) // lc
    w_t = kw_ref[...].T
    row_l = lax.broadcasted_iota(I32, (lc, QB), 0)
    qpos_l = lax.broadcasted_iota(I32, (lc, QB), 1) + q_lo

    def score_chunk(c, carry):
        base = pl.multiple_of(c * lc, lc)
        kc = ki_ref[pl.ds(base, lc), :]
        acc = jnp.zeros((lc, QB), F32)
        for h in range(IDX_HEADS):
            s = _dot_nt(kc, qi_ref[:, h * IDX_DIM:(h + 1) * IDX_DIM])
            acc = acc + w_t[IDX_DIM + h:IDX_DIM + h + 1, :] * jnp.maximum(s, 0.0)
        key = jnp.where(row_l + base <= qpos_l, _order_key(acc), INT_MIN)
        key_sc[pl.ds(base, lc), :] = key
        return carry

    lax.fori_loop(0, n_chunks, score_chunk, 0)

    def bisect(bit, thr):
        cand = thr + lax.shift_left(jnp.int32(1), 31 - bit)

        def count_chunk(c, cnt):
            base = pl.multiple_of(c * lc, lc)
            ge = (key_sc[pl.ds(base, lc), :] >= cand).astype(I32)
            return cnt + jnp.sum(ge.reshape(lc // SUBLANES, SUBLANES, QB), axis=0)

        cnt = lax.fori_loop(0, n_chunks, count_chunk, jnp.zeros((SUBLANES, QB), I32))
        tot = jnp.sum(cnt, axis=0, keepdims=True)
        return jnp.where(tot >= topk, cand, thr)

    thr = lax.fori_loop(0, 32, bisect, jnp.full((1, QB), INT_MIN, I32))
    thr = jnp.maximum(thr, INT_MIN + 1)

    m_sc[...] = jnp.full(m_sc.shape, NEG_BIG, F32)
    l_sc[...] = jnp.zeros(l_sc.shape, F32)
    acc_sc[...] = jnp.zeros(acc_sc.shape, F32)

    def attend(base, rows, sel, bias_of):
        kc = k_ref[pl.ds(base, rows), :]
        vc = v_ref[pl.ds(base, rows), :]
        for n in range(N_KV_HEADS):
            kn = kc[:, n * HEAD_DIM:(n + 1) * HEAD_DIM]
            vn = vc[:, n * HEAD_DIM:(n + 1) * HEAD_DIM]
            for g in range(GROUP):
                h = n * GROUP + g
                lg = _dot_nt(kn, q_ref[:, h * HEAD_DIM:(h + 1) * HEAD_DIM]) * ATTN_SCALE + bias_of(h)
                lg = jnp.where(sel, lg, NEG_BIG)
                m_old = m_sc[h:h + 1, :]
                m_new = jnp.maximum(m_old, jnp.max(lg, axis=0, keepdims=True))
                p = jnp.where(sel, jnp.exp(lg - m_new), 0.0)
                alpha = jnp.exp(m_old - m_new)
                l_sc[h:h + 1, :] = alpha * l_sc[h:h + 1, :] + jnp.sum(p, axis=0, keepdims=True)
                acc_sc[h] = alpha * acc_sc[h] + _dot_tn(vn, p.astype(BF16))
                m_sc[h:h + 1, :] = m_new

    far_hi = jnp.maximum(q_lo - QB, 0)
    n_far = (far_hi + lc - 1) // lc

    def far_chunk(c, carry):
        base = pl.multiple_of(c * lc, lc)
        sel = (key_sc[pl.ds(base, lc), :] >= thr) & (row_l + base < far_hi)
        attend(base, lc, sel, lambda h: cb_ref[h])
        return carry

    lax.fori_loop(0, n_far, far_chunk, 0)

    prev_lo = pl.multiple_of(jnp.maximum(q_lo - QB, 0), QB)
    sel_prev = (key_sc[pl.ds(prev_lo, QB), :] >= thr) & (i > 0)
    attend(prev_lo, QB, sel_prev, lambda h: pb_ref[h, :QB, :])
    diag_lo = pl.multiple_of(q_lo, QB)
    sel_diag = key_sc[pl.ds(diag_lo, QB), :] >= thr
    attend(diag_lo, QB, sel_diag, lambda h: pb_ref[h, QB:, :])

    for h in range(N_HEADS):
        o_t = acc_sc[h] / l_sc[h:h + 1, :]
        o_ref[:, h * HEAD_DIM:(h + 1) * HEAD_DIM] = o_t.T.astype(BF16)


def _prompt_attention(dm, q_b, qi_b, kw, ki_b, k_b, v_b, pb, cb):
    nq, tp = dm.nq, dm.tp
    topk = min(TOPK_MAX, dm.seq // 4)
    m = max(d for d in (1, 2, 3, 4) if nq % d == 0)
    lc = m * QB
    qblk = lambda w: pl.BlockSpec((QB, w), lambda b, i: (b * nq + i, 0))
    kblk = lambda w: pl.BlockSpec((tp, w), lambda b, i: (b, 0))
    return pl.pallas_call(
        functools.partial(_prompt_attn_kernel, topk=topk, lc=lc),
        grid=(dm.batch, nq),
        in_specs=[qblk(IDX_HEADS * IDX_DIM), qblk(ATTN_WIDTH), qblk(LANES),
                  kblk(IDX_DIM), kblk(KV_WIDTH), kblk(KV_WIDTH),
                  _const_spec(pb.shape), _const_spec(cb.shape)],
        out_specs=qblk(ATTN_WIDTH),
        out_shape=jax.ShapeDtypeStruct((dm.prompt_rows, ATTN_WIDTH), BF16),
        scratch_shapes=[pltpu.VMEM((tp, QB), I32), pltpu.VMEM((N_HEADS, HEAD_DIM, QB), F32),
                        pltpu.VMEM((N_HEADS, QB), F32), pltpu.VMEM((N_HEADS, QB), F32)],
        compiler_params=_params(("arbitrary", "arbitrary")),
        name="prompt_attention",
    )(qi_b, q_b, kw, ki_b, k_b, v_b, pb, cb)


def _decode_score_kernel(pt_ref, qi_ref, w_ref, kpage_ref, kcur_ref, key_ref, thr_ref, *, topk, n_pages, dec_seq):
    p = pl.program_id(1)

    def scores(keys_bf):
        s = _dot_nt(qi_ref[0], keys_bf)
        acc = jnp.zeros((SUBLANES, LANES), F32)
        for h in range(IDX_HEADS):
            rows = slice(h * SUBLANES, (h + 1) * SUBLANES)
            acc = acc + w_ref[0, rows, :] * jnp.maximum(s[rows, :], 0.0)
        return _order_key(acc)

    key_ref[0, pl.ds(p, 1)] = scores(kpage_ref[0].astype(BF16))[None]

    @pl.when(p == n_pages - 1)
    def _():
        r = lax.broadcasted_iota(I32, (SUBLANES, LANES), 0)
        c = lax.broadcasted_iota(I32, (SUBLANES, LANES), 1)
        cur = jnp.where((c <= r) & (c < dec_seq), scores(kcur_ref[0]), INT_MIN)
        key_ref[0, n_pages] = cur

        def bisect(bit, thr):
            cand = thr + lax.shift_left(jnp.int32(1), 31 - bit)

            def count_page(j, cnt):
                return cnt + (key_ref[0, j] >= cand).astype(I32)

            cnt = lax.fori_loop(0, n_pages + 1, count_page, jnp.zeros((SUBLANES, LANES), I32))
            tot = jnp.sum(cnt, axis=1, keepdims=True)
            return jnp.where(tot >= topk, cand, thr)

        thr = lax.fori_loop(0, 32, bisect, jnp.full((SUBLANES, 1), INT_MIN, I32))
        thr_ref[0] = jnp.broadcast_to(jnp.maximum(thr, INT_MIN + 1), (SUBLANES, LANES))


def _decode_attn_kernel(pt_ref, q_ref, kpage_ref, vpage_ref, key_ref, keycur_ref, thr_ref, kcur_ref, vcur_ref,
                        bfar_ref, bnear_ref, bcur_ref, o_ref, m_sc, l_sc, acc_sc, *, n_pages):
    p = pl.program_id(1)

    @pl.when(p == 0)
    def _():
        m_sc[...] = jnp.full(m_sc.shape, NEG_BIG, F32)
        l_sc[...] = jnp.zeros(l_sc.shape, F32)
        acc_sc[...] = jnp.zeros(acc_sc.shape, F32)

    thr4 = jnp.concatenate([thr_ref[0]] * GROUP, axis=0)

    def attend(keys8, k_bf, v_bf, bias_of):
        sel = jnp.concatenate([keys8] * GROUP, axis=0) >= thr4
        for n in range(N_KV_HEADS):
            sl = slice(n * HEAD_DIM, (n + 1) * HEAD_DIM)
            lg = _dot_nt(q_ref[0, n], k_bf[:, sl]) * ATTN_SCALE + bias_of(n)
            lg = jnp.where(sel, lg, NEG_BIG)
            m_old = m_sc[n]
            m_new = jnp.maximum(m_old, jnp.max(lg, axis=1, keepdims=True))
            pr = jnp.where(sel, jnp.exp(lg - m_new), 0.0)
            alpha = jnp.exp(m_old - m_new)
            l_sc[n] = alpha * l_sc[n] + jnp.sum(pr, axis=1, keepdims=True)
            acc_sc[n] = alpha * acc_sc[n] + _dot(pr.astype(BF16), v_bf[:, sl])
            m_sc[n] = m_new

    last = p == n_pages - 1
    attend(key_ref[0, 0], kpage_ref[0].astype(BF16), vpage_ref[0].astype(BF16),
           lambda n: jnp.where(last, bnear_ref[n], bfar_ref[n]))

    @pl.when(last)
    def _():
        attend(keycur_ref[0, 0], kcur_ref[0], vcur_ref[0], lambda n: bcur_ref[n])
        for n in range(N_KV_HEADS):
            o_ref[0, n] = acc_sc[n] / l_sc[n]


def _decode_attention(dm, page_table, cache_k, cache_v, cache_kidx, qi_s, w_col, ki_cur, q_s, k_cur, v_cur,
                      bfar, bnear, bcur):
    db, np_ = dm.dec_batch, dm.n_pages
    topk = min(TOPK_MAX, (dm.past_len + dm.dec_seq) // 4)
    rows_hq = IDX_HEADS * SUBLANES
    rows_gq = GROUP * SUBLANES
    per_b = lambda *shape: pl.BlockSpec((1,) + shape, lambda b, p, pt: (b,) + (0,) * len(shape))
    page = lambda w: pl.BlockSpec((1, PAGE_SIZE, w), lambda b, p, pt: (pt[b, p], 0, 0))
    const = lambda shape: pl.BlockSpec(shape, lambda b, p, pt: (0,) * len(shape))

    keys, thr = pl.pallas_call(
        functools.partial(_decode_score_kernel, topk=topk, n_pages=np_, dec_seq=dm.dec_seq),
        grid_spec=pltpu.PrefetchScalarGridSpec(
            num_scalar_prefetch=1, grid=(db, np_),
            in_specs=[per_b(rows_hq, IDX_DIM), per_b(rows_hq, 1), page(IDX_DIM), per_b(PAGE_SIZE, IDX_DIM)],
            out_specs=[per_b(np_ + 1, SUBLANES, LANES), per_b(SUBLANES, LANES)]),
        out_shape=[jax.ShapeDtypeStruct((db, np_ + 1, SUBLANES, LANES), I32),
                   jax.ShapeDtypeStruct((db, SUBLANES, LANES), I32)],
        compiler_params=_params(("arbitrary", "arbitrary")),
        name="decode_scores",
    )(page_table, qi_s, w_col, cache_kidx, ki_cur)

    key_page = pl.BlockSpec((1, 1, SUBLANES, LANES), lambda b, p, pt: (b, p, 0, 0))
    key_cur = pl.BlockSpec((1, 1, SUBLANES, LANES), lambda b, p, pt: (b, np_, 0, 0))
    return pl.pallas_call(
        functools.partial(_decode_attn_kernel, n_pages=np_),
        grid_spec=pltpu.PrefetchScalarGridSpec(
            num_scalar_prefetch=1, grid=(db, np_),
            in_specs=[per_b(N_KV_HEADS, rows_gq, HEAD_DIM), page(KV_WIDTH), page(KV_WIDTH), key_page, key_cur,
                      per_b(SUBLANES, LANES), per_b(PAGE_SIZE, KV_WIDTH), per_b(PAGE_SIZE, KV_WIDTH),
                      const(bfar.shape), const(bnear.shape), const(bcur.shape)],
            out_specs=per_b(N_KV_HEADS, rows_gq, HEAD_DIM),
            scratch_shapes=[pltpu.VMEM((N_KV_HEADS, rows_gq, 1), F32), pltpu.VMEM((N_KV_HEADS, rows_gq, 1), F32),
                            pltpu.VMEM((N_KV_HEADS, rows_gq, HEAD_DIM), F32)]),
        out_shape=jax.ShapeDtypeStruct((db, N_KV_HEADS, rows_gq, HEAD_DIM), F32),
        compiler_params=_params(("arbitrary", "arbitrary")),
        name="decode_attention",
    )(page_table, q_s, cache_k, cache_v, keys, keys, thr, k_cur, v_cur, bfar, bnear, bcur)


CONV_CT = 256


def _conv_ln_swish(ext_ref, rows, w_ref, b_ref, g_ref, beta_ref, y_sc, out_ref_setter):
    off = CONV_HALO - (CONV_W - 1)
    for ct in range(CONV_CH // CONV_CT):
        cs = slice(ct * CONV_CT, (ct + 1) * CONV_CT)
        acc = jnp.zeros((rows, CONV_CT), F32)
        for j in range(CONV_W):
            acc = acc + ext_ref[pl.ds(off + j, rows), cs] * w_ref[j:j + 1, cs]
        y_sc[:, cs] = acc + b_ref[:, cs]
    y = y_sc[...]
    mu = jnp.mean(y, axis=-1, keepdims=True)
    d = y - mu
    var = jnp.mean(d * d, axis=-1, keepdims=True)
    yn = d * lax.rsqrt(var + EPS) * g_ref[...] + beta_ref[...]
    out_ref_setter(yn * jax.nn.sigmoid(yn))


def _prompt_conv_kernel(prev_ref, cur_ref, w_ref, b_ref, g_ref, beta_ref, c_ref, ext_sc, y_sc):
    i = pl.program_id(1)
    halo = prev_ref[QB - CONV_HALO:, :]
    ext_sc[:CONV_HALO, :] = jnp.where(i > 0, halo, 0.0)
    ext_sc[CONV_HALO:, :] = cur_ref[...]

    def put(c):
        c_ref[...] = c.astype(BF16)

    _conv_ln_swish(ext_sc, QB, w_ref, b_ref, g_ref, beta_ref, y_sc, put)


def _sample_conv_kernel(ext_ref, w_ref, b_ref, g_ref, beta_ref, c_ref, y_sc, *, rows):
    def put(c):
        c_ref[0] = c

    _conv_ln_swish(ext_ref.at[0], rows, w_ref, b_ref, g_ref, beta_ref, y_sc, put)


def _conv_weights(dw_w, dw_b, ln_g, ln_b):
    return (dw_w, dw_b.reshape(1, CONV_CH), ln_g.reshape(1, CONV_CH), ln_b.reshape(1, CONV_CH))


def _prompt_conv(dm, a_all, cw):
    nq = dm.nq
    blk = lambda f: pl.BlockSpec((QB, CONV_CH), f)
    return pl.pallas_call(
        _prompt_conv_kernel,
        grid=(dm.batch, nq),
        in_specs=[blk(lambda b, i: (jnp.maximum(b * nq + i - 1, 0), 0)), blk(lambda b, i: (b * nq + i, 0)),
                  _const_spec((CONV_W, CONV_CH))] + [_const_spec((1, CONV_CH))] * 3,
        out_specs=blk(lambda b, i: (b * nq + i, 0)),
        out_shape=jax.ShapeDtypeStruct((dm.prompt_rows, CONV_CH), BF16),
        scratch_shapes=[pltpu.VMEM((CONV_HALO + QB, CONV_CH), F32), pltpu.VMEM((QB, CONV_CH), F32)],
        compiler_params=_params(("arbitrary", "arbitrary")),
        name="prompt_conv",
    )(a_all, a_all, *cw)


def _sample_conv(dm, ext_s, cw):
    rows = dm.dec_seq
    tot = CONV_HALO + rows
    return pl.pallas_call(
        functools.partial(_sample_conv_kernel, rows=rows),
        grid=(dm.dec_batch,),
        in_specs=[pl.BlockSpec((1, tot, CONV_CH), lambda b: (b, 0, 0)), _const_spec((CONV_W, CONV_CH))]
                 + [_const_spec((1, CONV_CH))] * 3,
        out_specs=pl.BlockSpec((1, rows, CONV_CH), lambda b: (b, 0, 0)),
        out_shape=jax.ShapeDtypeStruct((dm.dec_batch, rows, CONV_CH), F32),
        scratch_shapes=[pltpu.VMEM((rows, CONV_CH), F32)],
        compiler_params=_params(("arbitrary",)),
        name="sample_conv",
    )(ext_s, *cw)


def _outproj_kernel(o_ref, c_ref, x_ref, wo_ref, wc_ref, g_ref, wpq_ref, h_ref, hn_ref, qp_ref):
    h = x_ref[...] + _dot(o_ref[...], wo_ref[...]) + _dot(c_ref[...], wc_ref[...])
    h_ref[...] = h
    ms = jnp.mean(h * h, axis=-1, keepdims=True)
    hn = h * lax.rsqrt(ms + EPS) * g_ref[...]
    hn_ref[...] = hn
    qp_ref[...] = _dot(hn.astype(BF16), wpq_ref[...]).astype(BF16)


def _out_projection(o_all, c_all, x_all, w_out, g_ffn, peer_wq):
    rows = x_all.shape[0]
    wo = w_out[:ATTN_WIDTH].astype(BF16)
    wc = w_out[ATTN_WIDTH:].astype(BF16)
    wpq = peer_wq.astype(BF16)
    row = lambda w: pl.BlockSpec((ROW_TILE, w), lambda i: (i, 0))
    pq = PEER_HEADS * PEER_KDIM
    return pl.pallas_call(
        _outproj_kernel,
        grid=(rows // ROW_TILE,),
        in_specs=[row(ATTN_WIDTH), row(CONV_CH), row(D_MODEL), _weight_spec(wo.shape), _weight_spec(wc.shape),
                  _const_spec((1, D_MODEL)), _weight_spec(wpq.shape)],
        out_specs=[row(D_MODEL), row(D_MODEL), row(pq)],
        out_shape=[jax.ShapeDtypeStruct((rows, D_MODEL), F32), jax.ShapeDtypeStruct((rows, D_MODEL), F32),
                   jax.ShapeDtypeStruct((rows, pq), BF16)],
        compiler_params=_params(("arbitrary",)),
        name="out_projection",
    )(o_all, c_all, x_all, wo, wc, g_ffn.reshape(1, D_MODEL), wpq)


def _extract_top(vals, payload, count):
    n = vals.shape[0]
    pos = lax.broadcasted_iota(I32, vals.shape, 0)
    out_v, out_p = [], []
    for _ in range(count):
        m = jnp.max(vals, axis=0, keepdims=True)
        first = jnp.min(jnp.where(vals == m, pos, n), axis=0, keepdims=True)
        hit = pos == first
        out_v.append(m)
        out_p.append(jnp.sum(jnp.where(hit, payload, 0), axis=0, keepdims=True))
        vals = jnp.where(hit, -jnp.inf, vals)
    return jnp.concatenate(out_v, axis=0), jnp.concatenate(out_p, axis=0)


def _route_kernel(qp_ref, sk_ref, e_ref, g_ref):
    half = PEER_KDIM // 2
    key_id = lax.broadcasted_iota(I32, (PEER_NKEYS, LANES), 0)
    for h in range(PEER_HEADS):
        tops = []
        for c in range(2):
            qs = qp_ref[:, h * PEER_KDIM + c * half:h * PEER_KDIM + (c + 1) * half]
            s = _dot_nt(sk_ref[c], qs)
            tops.append(_extract_top(s, key_id, PEER_TOPK))
        (v1, i1), (v2, i2) = tops
        cand = jnp.concatenate([v1[a:a + 1, :] + v2 for a in range(PEER_TOPK)], axis=0)
        cidx = jnp.concatenate([i1[a:a + 1, :] * PEER_NKEYS + i2 for a in range(PEER_TOPK)], axis=0)
        vals, experts = _extract_top(cand, cidx, PEER_TOPK)
        ex = jnp.exp(vals - vals[0:1, :])
        g_ref[h] = ex / jnp.sum(ex, axis=0, keepdims=True)
        e_ref[h] = experts


def _peer_route(qp, sub_keys):
    rows = qp.shape[0]
    sk = sub_keys.astype(BF16)
    out = pl.BlockSpec((PEER_HEADS, PEER_TOPK, LANES), lambda i: (0, 0, i))
    return pl.pallas_call(
        _route_kernel,
        grid=(rows // LANES,),
        in_specs=[pl.BlockSpec((LANES, PEER_HEADS * PEER_KDIM), lambda i: (i, 0)), _const_spec(sk.shape)],
        out_specs=[out, out],
        out_shape=[jax.ShapeDtypeStruct((PEER_HEADS, PEER_TOPK, rows), I32),
                   jax.ShapeDtypeStruct((PEER_HEADS, PEER_TOPK, rows), F32)],
        compiler_params=_params(("arbitrary",)),
        name="peer_route",
    )(qp, sk)


def _peer_ffn_kernel(idx_ref, gate_ref, hn_ref, h_ref, uv_ref, y_ref, buf, out_sc, sem):
    n_groups = PEER_TOK_BLOCK // PEER_UNROLL

    def issue(tok, slot):
        for k in range(PEER_SLOTS):
            pltpu.make_async_copy(uv_ref.at[idx_ref[tok, k]], buf.at[slot, k], sem.at[slot]).start()

    def wait(slot):
        pltpu.make_async_copy(uv_ref.at[pl.ds(0, PEER_SLOTS)], buf.at[slot], sem.at[slot]).wait()

    issue(0, 0)

    def group(gi, carry):
        t0 = pl.multiple_of(gi * PEER_UNROLL, PEER_UNROLL)
        gates_t = gate_ref[pl.ds(t0, PEER_UNROLL), :].T
        x_grp = hn_ref[pl.ds(t0, PEER_UNROLL), :]
        for j in range(PEER_UNROLL):
            slot = j % 2
            tok = t0 + j
            if j + 1 < PEER_UNROLL:
                issue(tok + 1, 1 - slot)
            else:
                @pl.when(gi + 1 < n_groups)
                def _():
                    issue(tok + 1, 1 - slot)
            wait(slot)
            hacc = jnp.zeros((PEER_SLOTS, LANES), F32)
            for c in range(D_MODEL // LANES):
                cs = slice(c * LANES, (c + 1) * LANES)
                hacc = hacc + buf[slot, :, cs] * x_grp[j:j + 1, cs]
            hcol = jnp.sum(hacc, axis=1, keepdims=True)
            act = gates_t[:, j:j + 1] * jax.nn.gelu(hcol)
            act_b = jnp.broadcast_to(act, (PEER_SLOTS, LANES))
            for c in range(D_MODEL // LANES):
                cs = slice(c * LANES, (c + 1) * LANES)
                vs = slice(D_MODEL + c * LANES, D_MODEL + (c + 1) * LANES)
                out_sc[j:j + 1, cs] = jnp.sum(buf[slot, :, vs] * act_b, axis=0, keepdims=True)
        y_ref[pl.ds(t0, PEER_UNROLL), :] = h_ref[pl.ds(t0, PEER_UNROLL), :] + out_sc[...]
        return carry

    lax.fori_loop(0, n_groups, group, 0)


def _peer_ffn(experts_tok, gates_tok, hn2, h, uv):
    rows = h.shape[0]
    tb = PEER_TOK_BLOCK
    row = lambda w: pl.BlockSpec((tb, w), lambda i: (i, 0))
    return pl.pallas_call(
        _peer_ffn_kernel,
        grid=(rows // tb,),
        in_specs=[pl.BlockSpec((tb, PEER_SLOTS), lambda i: (i, 0), memory_space=pltpu.SMEM),
                  row(PEER_SLOTS), row(D_MODEL), row(D_MODEL), pl.BlockSpec(memory_space=pl.ANY)],
        out_specs=row(D_MODEL),
        out_shape=jax.ShapeDtypeStruct((rows, D_MODEL), F32),
        scratch_shapes=[pltpu.VMEM((2, PEER_SLOTS, 2 * D_MODEL), F32), pltpu.VMEM((PEER_UNROLL, D_MODEL), F32),
                        pltpu.SemaphoreType.DMA((2,))],
        compiler_params=pltpu.CompilerParams(dimension_semantics=("arbitrary",), vmem_limit_bytes=VMEM_LIMIT,
                                             disable_bounds_checks=True),
        name="peer_ffn",
    )(experts_tok, gates_tok, hn2, h, uv)


def _step(dm, x_prompt, x_sample, cache_k, cache_v, cache_kidx, state_conv, page_table, meta_tokens, rel_bias,
          g_attn, w_in, g_q, g_k, conv_dw_w, conv_dw_b, conv_ln_g, conv_ln_b, w_out, g_ffn, peer_wq, peer_subkeys,
          peer_u, peer_v):
    bsz, t, tp, db, ds = dm.batch, dm.t_prompt, dm.tp, dm.dec_batch, dm.dec_seq
    pr, sr = dm.prompt_rows, dm.sample_rows

    meta = jnp.broadcast_to(meta_tokens[None].astype(x_prompt.dtype), (bsz, N_META, D_MODEL))
    xp = jnp.concatenate([meta, x_prompt, jnp.zeros((bsz, tp - t, D_MODEL), x_prompt.dtype)], axis=1)
    x_all = jnp.concatenate([xp.reshape(pr, D_MODEL), x_sample.reshape(sr, D_MODEL),
                             jnp.zeros((dm.rows - pr - sr, D_MODEL), x_prompt.dtype)], axis=0)

    q_b, k_f, v_f, k_b, v_b, qi_b, kw, ki_b, a_all = _in_projection(x_all, g_attn[0], w_in[0], g_q[0], g_k[0])

    pb, dn, dc = _bias_tiles(rel_bias)
    far = rel_bias[N_BUCKETS - 1]
    cb = jnp.broadcast_to(far[:, None, None], (N_HEADS, 1, QB))
    o_p = _prompt_attention(dm, q_b, qi_b, kw, ki_b, k_b, v_b, pb, cb)

    s0, s1 = pr, pr + sr
    qi_s = qi_b[s0:s1].reshape(db, ds, IDX_HEADS, IDX_DIM).transpose(0, 2, 1, 3).reshape(db, IDX_HEADS * ds, IDX_DIM)
    w_col = kw[s0:s1, IDX_DIM:IDX_DIM + IDX_HEADS].reshape(db, ds, IDX_HEADS).transpose(0, 2, 1)
    w_col = w_col.reshape(db, IDX_HEADS * ds, 1)
    q_s = q_b[s0:s1].reshape(db, ds, N_KV_HEADS, GROUP, HEAD_DIM).transpose(0, 2, 3, 1, 4)
    q_s = q_s.reshape(db, N_KV_HEADS, GROUP * ds, HEAD_DIM)
    pad_keys = lambda z: jnp.pad(z.reshape(db, ds, -1), ((0, 0), (0, PAGE_SIZE - ds), (0, 0)))
    by_group = lambda z: z.reshape(N_KV_HEADS, GROUP * SUBLANES, LANES)
    bfar = jnp.broadcast_to(far.reshape(N_KV_HEADS, GROUP, 1, 1), (N_KV_HEADS, GROUP, SUBLANES, LANES))
    o_s = _decode_attention(
        dm, page_table, cache_k[0].reshape(-1, PAGE_SIZE, KV_WIDTH), cache_v[0].reshape(-1, PAGE_SIZE, KV_WIDTH),
        cache_kidx[0], qi_s, w_col, pad_keys(ki_b[s0:s1]), q_s, pad_keys(k_b[s0:s1]), pad_keys(v_b[s0:s1]),
        by_group(bfar), by_group(dn), by_group(dc))
    o_s = o_s.reshape(db, N_KV_HEADS, GROUP, ds, HEAD_DIM).transpose(0, 3, 1, 2, 4).reshape(sr, ATTN_WIDTH)

    cw = _conv_weights(conv_dw_w[0], conv_dw_b[0], conv_ln_g[0], conv_ln_b[0])
    c_p = _prompt_conv(dm, a_all, cw)
    a_s = a_all[s0:s1].reshape(db, ds, CONV_CH)
    ext_s = jnp.concatenate([jnp.zeros((db, CONV_HALO - (CONV_W - 1), CONV_CH), F32), state_conv[0], a_s], axis=1)
    c_s = _sample_conv(dm, ext_s, cw).reshape(sr, CONV_CH)

    tail = jnp.zeros((dm.rows - pr - sr, ATTN_WIDTH), BF16)
    o_all = jnp.concatenate([o_p, o_s.astype(BF16), tail], axis=0)
    c_all = jnp.concatenate([c_p, c_s.astype(BF16), tail], axis=0)
    h, hn2, qp = _out_projection(o_all, c_all, x_all, w_out[0], g_ffn[0], peer_wq[0])

    experts, gates = _peer_route(qp, peer_subkeys[0])
    experts_tok = experts.reshape(PEER_SLOTS, dm.rows).T
    gates_tok = gates.reshape(PEER_SLOTS, dm.rows).T
    uv = jnp.concatenate([peer_u[0], peer_v[0]], axis=1)
    y = _peer_ffn(experts_tok, gates_tok, hn2, h, uv)

    per_seq = lambda z, w: z[:pr].reshape(bsz, tp, w)[:, :t]
    y_prompt = per_seq(y, D_MODEL)[:, N_META:]
    y_sample = y[s0:s1].reshape(db, ds, D_MODEL)
    kv_shape = (N_KV_HEADS, HEAD_DIM)
    new_k_p = per_seq(k_f, KV_WIDTH).reshape(1, bsz, t, *kv_shape)
    new_v_p = per_seq(v_f, KV_WIDTH).reshape(1, bsz, t, *kv_shape)
    new_ki_p = per_seq(kw, LANES)[:, :, :IDX_DIM][None]
    new_conv_p = per_seq(a_all, CONV_CH)[:, t - (CONV_W - 1):][None]
    new_k_s = k_f[s0:s1].reshape(1, db, ds, *kv_shape)
    new_v_s = v_f[s0:s1].reshape(1, db, ds, *kv_shape)
    new_ki_s = kw[s0:s1, :IDX_DIM].reshape(1, db, ds, IDX_DIM)
    new_conv_s = ext_s[:, -(CONV_W - 1):][None]
    return (y_prompt, y_sample, new_k_p, new_v_p, new_ki_p, new_conv_p, new_k_s, new_v_s, new_ki_s, new_conv_s)


def kernel(x_prompt, x_sample, cache_k, cache_v, cache_kidx, state_conv, page_table, meta_tokens, rel_bias, g_attn,
           w_in, g_q, g_k, conv_dw_w, conv_dw_b, conv_ln_g, conv_ln_b, w_out, g_ffn, peer_wq, peer_subkeys, peer_u,
           peer_v):
    assert g_attn.shape[0] == 1, "single trunk layer"
    dm = Dims(batch=x_prompt.shape[0], seq=x_prompt.shape[1], dec_batch=x_sample.shape[0],
              dec_seq=x_sample.shape[1], past_len=page_table.shape[1] * PAGE_SIZE)
    assert dm.dec_seq == SUBLANES and dm.t_prompt >= CONV_W - 1
    return _step(dm, x_prompt, x_sample, cache_k, cache_v, cache_kidx, state_conv, page_table, meta_tokens, rel_bias,
                 g_attn, w_in, g_q, g_k, conv_dw_w, conv_dw_b, conv_ln_g, conv_ln_b, w_out, g_ffn, peer_wq,
                 peer_subkeys, peer_u, peer_v)
```

```python
import functools
import math
from typing import NamedTuple

import jax
import jax.numpy as jnp
from jax import lax
from jax.experimental import pallas as pl
from jax.experimental.pallas import tpu as pltpu

F32 = jnp.float32
BF16 = jnp.bfloat16
I32 = jnp.int32

D_MODEL = 2048
N_META = 16
N_HEADS = 8
HEAD_DIM = 128
N_KV_HEADS = 2
GROUP = N_HEADS // N_KV_HEADS
ATTN_WIDTH = N_HEADS * HEAD_DIM
KV_WIDTH = N_KV_HEADS * HEAD_DIM
CONV_CH = D_MODEL - ATTN_WIDTH
CONV_W = 31
IDX_HEADS = 16
IDX_DIM = 64
IDX_SCALE = (IDX_HEADS * IDX_DIM) ** -0.5
TOPK_MAX = 256
N_BUCKETS = 32
MAX_DISTANCE = 128
ATTN_SCALE = HEAD_DIM ** -0.5
PAGE_SIZE = 128
PEER_HEADS = 8
PEER_NKEYS = 128
PEER_KDIM = 128
PEER_TOPK = 16
PEER_SLOTS = PEER_HEADS * PEER_TOPK
EPS = 1e-6

LANES = 128
SUBLANES = 8
QB = 128
ROW_TILE = 256
CONV_HALO = 32
PEER_TOK_BLOCK = 64
PEER_UNROLL = 8
INT_MIN = -(2 ** 31)
NEG_BIG = -1e30
VMEM_LIMIT = 56 * 1024 * 1024

NT_DIMS = (((1,), (1,)), ((), ()))
TN_DIMS = (((0,), (0,)), ((), ()))


class Dims(NamedTuple):
    batch: int
    seq: int
    dec_batch: int
    dec_seq: int
    past_len: int

    @property
    def t_prompt(self):
        return self.seq + N_META

    @property
    def nq(self):
        return -(-self.t_prompt // QB)

    @property
    def tp(self):
        return self.nq * QB

    @property
    def prompt_rows(self):
        return self.batch * self.tp

    @property
    def sample_rows(self):
        return self.dec_batch * self.dec_seq

    @property
    def rows(self):
        r = self.prompt_rows + self.sample_rows
        return -(-r // ROW_TILE) * ROW_TILE

    @property
    def n_pages(self):
        return self.past_len // PAGE_SIZE


def _dot(a, b):
    return jnp.dot(a, b, preferred_element_type=F32)


def _dot_nt(a, b):
    return lax.dot_general(a, b, NT_DIMS, preferred_element_type=F32)


def _dot_tn(a, b):
    return lax.dot_general(a, b, TN_DIMS, preferred_element_type=F32)


def _params(sem):
    return pltpu.CompilerParams(dimension_semantics=sem, vmem_limit_bytes=VMEM_LIMIT)


def _const_spec(shape):
    nd = len(shape)
    return pl.BlockSpec(shape, lambda *_: (0,) * nd)


def _weight_spec(shape):
    nd = len(shape)
    return pl.BlockSpec(shape, lambda *_: (0,) * nd, pipeline_mode=pl.Buffered(1))


def _order_key(x):
    bits = pltpu.bitcast(x + 0.0, I32)
    return bits ^ ((bits >> 31) & 0x7FFFFFFF)


def _inproj_kernel(x_ref, ga_ref, wq_ref, wkv_ref, wqi_ref, wkw_ref, wu_ref, gq_ref, gk_ref,
                   q_ref, kf_ref, vf_ref, kb_ref, vb_ref, qi_ref, kw_ref, kib_ref, a_ref):
    x = x_ref[...]
    ms = jnp.mean(x * x, axis=-1, keepdims=True)
    hn = (x * lax.rsqrt(ms + EPS) * ga_ref[...]).astype(BF16)

    def head_norm(z, g):
        zm = jnp.mean(z * z, axis=-1, keepdims=True)
        return z * lax.rsqrt(zm + EPS) * g

    q = _dot(hn, wq_ref[...])
    for h in range(N_HEADS):
        sl = slice(h * HEAD_DIM, (h + 1) * HEAD_DIM)
        q_ref[:, sl] = head_norm(q[:, sl], gq_ref[...]).astype(BF16)
    kv = _dot(hn, wkv_ref[...])
    for n in range(N_KV_HEADS):
        sl = slice(n * HEAD_DIM, (n + 1) * HEAD_DIM)
        kn = head_norm(kv[:, sl], gk_ref[...])
        kf_ref[:, sl] = kn
        kb_ref[:, sl] = kn.astype(BF16)
    v = kv[:, KV_WIDTH:]
    vf_ref[...] = v
    vb_ref[...] = v.astype(BF16)
    qi_ref[...] = _dot(hn, wqi_ref[...]).astype(BF16)
    kw = _dot(hn, wkw_ref[...])
    lane = lax.broadcasted_iota(I32, kw.shape, 1)
    kw = jnp.where(lane >= IDX_DIM, kw * IDX_SCALE, kw)
    kw_ref[...] = kw
    kib_ref[...] = kw[:, :IDX_DIM].astype(BF16)
    u = _dot(hn, wu_ref[...])
    a_ref[...] = u[:, :CONV_CH] * jax.nn.sigmoid(u[:, CONV_CH:])


def _in_projection(x_all, g_attn, w_in, g_q, g_k):
    rows = x_all.shape[0]
    o_q = ATTN_WIDTH
    o_k = o_q + KV_WIDTH
    o_v = o_k + KV_WIDTH
    o_qi = o_v + IDX_HEADS * IDX_DIM
    o_ki = o_qi + IDX_DIM
    o_wi = o_ki + IDX_HEADS
    wq = w_in[:, :o_q].astype(BF16)
    wkv = w_in[:, o_q:o_v].astype(BF16)
    wqi = w_in[:, o_v:o_qi].astype(BF16)
    wkw = jnp.pad(w_in[:, o_qi:o_wi], ((0, 0), (0, LANES - IDX_DIM - IDX_HEADS))).astype(BF16)
    wu = w_in[:, o_wi:].astype(BF16)
    row = lambda w: pl.BlockSpec((ROW_TILE, w), lambda i: (i, 0))
    outs = [
        (ATTN_WIDTH, BF16), (KV_WIDTH, F32), (KV_WIDTH, F32), (KV_WIDTH, BF16), (KV_WIDTH, BF16),
        (IDX_HEADS * IDX_DIM, BF16), (LANES, F32), (IDX_DIM, BF16), (CONV_CH, F32),
    ]
    return pl.pallas_call(
        _inproj_kernel,
        grid=(rows // ROW_TILE,),
        in_specs=[row(D_MODEL), _const_spec((1, D_MODEL)), _weight_spec(wq.shape), _weight_spec(wkv.shape),
                  _weight_spec(wqi.shape), _weight_spec(wkw.shape), _weight_spec(wu.shape),
                  _const_spec((1, HEAD_DIM)), _const_spec((1, HEAD_DIM))],
        out_specs=[row(w) for w, _ in outs],
        out_shape=[jax.ShapeDtypeStruct((rows, w), dt) for w, dt in outs],
        compiler_params=_params(("arbitrary",)),
        name="in_projection",
    )(x_all, g_attn.reshape(1, D_MODEL), wq, wkv, wqi, wkw, wu, g_q.reshape(1, HEAD_DIM), g_k.reshape(1, HEAD_DIM))


def _t5_bias_of(dist, rb_ref, h):
    n = jnp.maximum(dist, 0)
    max_exact = N_BUCKETS // 2
    nf = jnp.maximum(n, 1).astype(F32)
    large = max_exact + (jnp.log(nf / max_exact) / math.log(MAX_DISTANCE / max_exact)
                         * (N_BUCKETS - max_exact)).astype(I32)
    large = jnp.minimum(large, N_BUCKETS - 1)
    bucket = jnp.where(n < max_exact, n, large)
    out = jnp.zeros(dist.shape, F32)
    for k in range(N_BUCKETS):
        out = jnp.where(bucket == k, rb_ref[k, h], out)
    return out


def _bias_kernel(rb_ref, pb_ref, dn_ref, dc_ref):
    r2 = lax.broadcasted_iota(I32, (2 * QB, QB), 0)
    c2 = lax.broadcasted_iota(I32, (2 * QB, QB), 1)
    r1 = lax.broadcasted_iota(I32, (SUBLANES, LANES), 0)
    c1 = lax.broadcasted_iota(I32, (SUBLANES, LANES), 1)
    for h in range(N_HEADS):
        pb_ref[h] = _t5_bias_of(QB + c2 - r2, rb_ref, h)
        dn_ref[h] = _t5_bias_of(PAGE_SIZE + r1 - c1, rb_ref, h)
        dc_ref[h] = _t5_bias_of(r1 - c1, rb_ref, h)


def _bias_tiles(rel_bias):
    return pl.pallas_call(
        _bias_kernel,
        in_specs=[pl.BlockSpec(memory_space=pltpu.SMEM)],
        out_shape=[jax.ShapeDtypeStruct((N_HEADS, 2 * QB, QB), F32),
                   jax.ShapeDtypeStruct((N_HEADS, SUBLANES, LANES), F32),
                   jax.ShapeDtypeStruct((N_HEADS, SUBLANES, LANES), F32)],
        name="t5_bias_tiles",
    )(rel_bias)


def _prompt_attn_kernel(qi_ref, q_ref, kw_ref, ki_ref, k_ref, v_ref, pb_ref, cb_ref, o_ref,
                        key_sc, acc_sc, m_sc, l_sc, *, topk, lc):
    i = pl.program_id(1)
    q_lo = i * QB
    n_chunks = (q_lo + QB + lc - 1) // lc
    w_t = kw_ref[...].T
    row_l = lax.broadcasted_iota(I32, (lc, QB), 0)
    qpos_l = lax.broadcasted_iota(I32, (lc, QB), 1) + q_lo

    def score_chunk(c, carry):
        base = pl.multiple_of(c * lc, lc)
        kc = ki_ref[pl.ds(base, lc), :]
        acc = jnp.zeros((lc, QB), F32)
        for h in range(IDX_HEADS):
            s = _dot_nt(kc, qi_ref[:, h * IDX_DIM:(h + 1) * IDX_DIM])
            acc = acc + w_t[IDX_DIM + h:IDX_DIM + h + 1, :] * jnp.maximum(s, 0.0)
        key = jnp.where(row_l + base <= qpos_l, _order_key(acc), INT_MIN)
        key_sc[pl.ds(base, lc), :] = key
        return carry

    lax.fori_loop(0, n_chunks, score_chunk, 0)

    def count_keys(pred):
        def count_chunk(c, cnt):
            base = pl.multiple_of(c * lc, lc)
            hit = pred(key_sc[pl.ds(base, lc), :], row_l + base).astype(I32)
            return cnt + jnp.sum(hit.reshape(lc // SUBLANES, SUBLANES, QB), axis=0)

        cnt = lax.fori_loop(0, n_chunks, count_chunk, jnp.zeros((SUBLANES, QB), I32))
        return jnp.sum(cnt, axis=0, keepdims=True)

    def bisect(bit, thr):
        cand = thr + lax.shift_left(jnp.int32(1), 31 - bit)
        return jnp.where(count_keys(lambda k, l: k >= cand) >= topk, cand, thr)

    thr = lax.fori_loop(0, 32, bisect, jnp.full((1, QB), INT_MIN, I32))

    surplus = count_keys(lambda k, l: k >= thr) - topk
    tied = (surplus > 0) & (thr > INT_MIN)

    @pl.when(jnp.max(tied.astype(I32)) > 0)
    def _():
        need = topk - count_keys(lambda k, l: k > thr)
        pos_bits = key_sc.shape[0].bit_length()

        def bisect_pos(bit, cut):
            cand = cut + lax.shift_left(jnp.int32(1), pos_bits - 1 - bit)
            return jnp.where(count_keys(lambda k, l: (k == thr) & (l < cand)) < need, cand, cut)

        cut = lax.fori_loop(0, pos_bits, bisect_pos, jnp.zeros((1, QB), I32))

        def drop_chunk(c, carry):
            base = pl.multiple_of(c * lc, lc)
            k = key_sc[pl.ds(base, lc), :]
            drop = tied & (k == thr) & (row_l + base > cut)
            key_sc[pl.ds(base, lc), :] = jnp.where(drop, INT_MIN, k)
            return carry

        lax.fori_loop(0, n_chunks, drop_chunk, 0)

    thr = jnp.maximum(thr, INT_MIN + 1)

    m_sc[...] = jnp.full(m_sc.shape, NEG_BIG, F32)
    l_sc[...] = jnp.zeros(l_sc.shape, F32)
    acc_sc[...] = jnp.zeros(acc_sc.shape, F32)

    def attend(base, rows, sel, bias_of):
        kc = k_ref[pl.ds(base, rows), :]
        vc = v_ref[pl.ds(base, rows), :]
        for n in range(N_KV_HEADS):
            kn = kc[:, n * HEAD_DIM:(n + 1) * HEAD_DIM]
            vn = vc[:, n * HEAD_DIM:(n + 1) * HEAD_DIM]
            for g in range(GROUP):
                h = n * GROUP + g
                lg = _dot_nt(kn, q_ref[:, h * HEAD_DIM:(h + 1) * HEAD_DIM]) * ATTN_SCALE + bias_of(h)
                lg = jnp.where(sel, lg, NEG_BIG)
                m_old = m_sc[h:h + 1, :]
                m_new = jnp.maximum(m_old, jnp.max(lg, axis=0, keepdims=True))
                p = jnp.where(sel, jnp.exp(lg - m_new), 0.0)
                alpha = jnp.exp(m_old - m_new)
                l_sc[h:h + 1, :] = alpha * l_sc[h:h + 1, :] + jnp.sum(p, axis=0, keepdims=True)
                acc_sc[h] = alpha * acc_sc[h] + _dot_tn(vn, p.astype(BF16))
                m_sc[h:h + 1, :] = m_new

    far_hi = jnp.maximum(q_lo - QB, 0)
    n_far = (far_hi + lc - 1) // lc

    def far_chunk(c, carry):
        base = pl.multiple_of(c * lc, lc)
        sel = (key_sc[pl.ds(base, lc), :] >= thr) & (row_l + base < far_hi)
        attend(base, lc, sel, lambda h: cb_ref[h])
        return carry

    lax.fori_loop(0, n_far, far_chunk, 0)

    prev_lo = pl.multiple_of(jnp.maximum(q_lo - QB, 0), QB)
    sel_prev = (key_sc[pl.ds(prev_lo, QB), :] >= thr) & (i > 0)
    attend(prev_lo, QB, sel_prev, lambda h: pb_ref[h, :QB, :])
    diag_lo = pl.multiple_of(q_lo, QB)
    sel_diag = key_sc[pl.ds(diag_lo, QB), :] >= thr
    attend(diag_lo, QB, sel_diag, lambda h: pb_ref[h, QB:, :])

    for h in range(N_HEADS):
        o_t = acc_sc[h] / l_sc[h:h + 1, :]
        o_ref[:, h * HEAD_DIM:(h + 1) * HEAD_DIM] = o_t.T.astype(BF16)


def _prompt_attention(dm, q_b, qi_b, kw, ki_b, k_b, v_b, pb, cb):
    nq, tp = dm.nq, dm.tp
    topk = min(TOPK_MAX, dm.seq // 4)
    m = max(d for d in (1, 2, 3, 4) if nq % d == 0)
    lc = m * QB
    qblk = lambda w: pl.BlockSpec((QB, w), lambda b, i: (b * nq + i, 0))
    kblk = lambda w: pl.BlockSpec((tp, w), lambda b, i: (b, 0))
    return pl.pallas_call(
        functools.partial(_prompt_attn_kernel, topk=topk, lc=lc),
        grid=(dm.batch, nq),
        in_specs=[qblk(IDX_HEADS * IDX_DIM), qblk(ATTN_WIDTH), qblk(LANES),
                  kblk(IDX_DIM), kblk(KV_WIDTH), kblk(KV_WIDTH),
                  _const_spec(pb.shape), _const_spec(cb.shape)],
        out_specs=qblk(ATTN_WIDTH),
        out_shape=jax.ShapeDtypeStruct((dm.prompt_rows, ATTN_WIDTH), BF16),
        scratch_shapes=[pltpu.VMEM((tp, QB), I32), pltpu.VMEM((N_HEADS, HEAD_DIM, QB), F32),
                        pltpu.VMEM((N_HEADS, QB), F32), pltpu.VMEM((N_HEADS, QB), F32)],
        compiler_params=_params(("arbitrary", "arbitrary")),
        name="prompt_attention",
    )(qi_b, q_b, kw, ki_b, k_b, v_b, pb, cb)


DEC_PAGES_MAX = 16


def _decode_kernel(pt_ref, qi_ref, w_ref, q_ref, kicur_ref, kcur_ref, vcur_ref, bfar_ref, bnear_ref, bcur_ref,
                   *refs, topk, n_pages, dec_seq, pps):
    ki_refs = refs[:pps]
    k_refs = refs[pps:2 * pps]
    v_refs = refs[2 * pps:3 * pps]
    o_ref, key_sc, thr_sc, m_sc, l_sc, acc_sc = refs[3 * pps:]
    n_groups = n_pages // pps
    s = pl.program_id(1)

    def scores(keys_bf):
        sc = _dot_nt(qi_ref[0], keys_bf)
        acc = jnp.zeros((SUBLANES, LANES), F32)
        for h in range(IDX_HEADS):
            rows = slice(h * SUBLANES, (h + 1) * SUBLANES)
            acc = acc + w_ref[0, rows, :] * jnp.maximum(sc[rows, :], 0.0)
        return _order_key(acc)

    @pl.when(s < n_groups)
    def _():
        for j in range(pps):
            key_sc[s * pps + j] = scores(ki_refs[j][...].astype(BF16))

    @pl.when(s == n_groups - 1)
    def _():
        r = lax.broadcasted_iota(I32, (SUBLANES, LANES), 0)
        c = lax.broadcasted_iota(I32, (SUBLANES, LANES), 1)
        key_sc[n_pages] = jnp.where((c <= r) & (c < dec_seq), scores(kicur_ref[0]), INT_MIN)

        def count_keys(pred):
            def count_page(j, cnt):
                return cnt + pred(key_sc[j], c + j * PAGE_SIZE).astype(I32)

            cnt = lax.fori_loop(0, n_pages + 1, count_page, jnp.zeros((SUBLANES, LANES), I32))
            return jnp.sum(cnt, axis=1, keepdims=True)

        def bisect(bit, thr):
            cand = thr + lax.shift_left(jnp.int32(1), 31 - bit)
            return jnp.where(count_keys(lambda k, l: k >= cand) >= topk, cand, thr)

        thr = lax.fori_loop(0, 32, bisect, jnp.full((SUBLANES, 1), INT_MIN, I32))

        surplus = count_keys(lambda k, l: k >= thr) - topk
        tied = (surplus > 0) & (thr > INT_MIN)

        @pl.when(jnp.max(tied.astype(I32)) > 0)
        def _():
            need = topk - count_keys(lambda k, l: k > thr)
            pos_bits = ((n_pages + 1) * PAGE_SIZE).bit_length()

            def bisect_pos(bit, cut):
                cand = cut + lax.shift_left(jnp.int32(1), pos_bits - 1 - bit)
                return jnp.where(count_keys(lambda k, l: (k == thr) & (l < cand)) < need, cand, cut)

            cut = lax.fori_loop(0, pos_bits, bisect_pos, jnp.zeros((SUBLANES, 1), I32))

            def drop_page(j, carry):
                k = key_sc[j]
                drop = tied & (k == thr) & (c + j * PAGE_SIZE > cut)
                key_sc[j] = jnp.where(drop, INT_MIN, k)
                return carry

            lax.fori_loop(0, n_pages + 1, drop_page, 0)

        thr_sc[...] = jnp.broadcast_to(jnp.maximum(thr, INT_MIN + 1), (SUBLANES, LANES))
        m_sc[...] = jnp.full(m_sc.shape, NEG_BIG, F32)
        l_sc[...] = jnp.zeros(l_sc.shape, F32)
        acc_sc[...] = jnp.zeros(acc_sc.shape, F32)

    def attend(key_tiles, k_of, v_of, bias_of):
        thr4 = jnp.concatenate([thr_sc[...]] * GROUP, axis=0)
        sels = [jnp.concatenate([kt] * GROUP, axis=0) >= thr4 for kt in key_tiles]
        for n in range(N_KV_HEADS):
            lgs = []
            for j, sel in enumerate(sels):
                lg = _dot_nt(q_ref[0, n], k_of(j, n)) * ATTN_SCALE + bias_of(j, n)
                lgs.append(jnp.where(sel, lg, NEG_BIG))
            m_old = m_sc[n]
            m_new = jnp.maximum(m_old, jnp.max(functools.reduce(jnp.maximum, lgs), axis=1, keepdims=True))
            prs = [jnp.where(sel, jnp.exp(lg - m_new), 0.0) for lg, sel in zip(lgs, sels)]
            alpha = jnp.exp(m_old - m_new)
            l_sc[n] = alpha * l_sc[n] + jnp.sum(functools.reduce(jnp.add, prs), axis=1, keepdims=True)
            pv = functools.reduce(jnp.add, [_dot(pr.astype(BF16), v_of(j, n)) for j, pr in enumerate(prs)])
            acc_sc[n] = alpha * acc_sc[n] + pv
            m_sc[n] = m_new

    @pl.when(s >= n_groups)
    def _():
        g = s - n_groups
        last_group = g == n_groups - 1

        def bias_of(j, n):
            if j == pps - 1:
                return jnp.where(last_group, bnear_ref[n], bfar_ref[n])
            return bfar_ref[n]

        attend([key_sc[g * pps + j] for j in range(pps)],
               lambda j, n: k_refs[j][:, n, :].astype(BF16), lambda j, n: v_refs[j][:, n, :].astype(BF16), bias_of)

    @pl.when(s == 2 * n_groups - 1)
    def _():
        sl = lambda n: slice(n * HEAD_DIM, (n + 1) * HEAD_DIM)
        attend([key_sc[n_pages]], lambda j, n: kcur_ref[0][:, sl(n)], lambda j, n: vcur_ref[0][:, sl(n)],
               lambda j, n: bcur_ref[n])
        for n in range(N_KV_HEADS):
            o_ref[0, n] = acc_sc[n] / l_sc[n]


def _decode_attention(dm, page_table, cache_k, cache_v, cache_kidx, qi_s, w_col, ki_cur, q_s, k_cur, v_cur,
                      bfar, bnear, bcur):
    db, np_ = dm.dec_batch, dm.n_pages
    topk = min(TOPK_MAX, (dm.past_len + dm.dec_seq) // 4)
    rows_hq = IDX_HEADS * SUBLANES
    rows_gq = GROUP * SUBLANES
    pps = max(d for d in range(1, DEC_PAGES_MAX + 1) if np_ % d == 0)
    ng = np_ // pps
    per_b = lambda *shape: pl.BlockSpec((1,) + shape, lambda b, s, pt: (b,) + (0,) * len(shape))
    const = lambda shape: pl.BlockSpec(shape, lambda b, s, pt: (0,) * len(shape))
    ki_page = lambda j: pl.BlockSpec(
        (None, None, PAGE_SIZE, IDX_DIM), lambda b, s, pt: (0, pt[b, jnp.minimum(s, ng - 1) * pps + j], 0, 0))
    kv_page = lambda j: pl.BlockSpec(
        (None, None, PAGE_SIZE, N_KV_HEADS, HEAD_DIM),
        lambda b, s, pt: (0, pt[b, jnp.maximum(s - ng, 0) * pps + j], 0, 0, 0))
    pages = range(pps)
    return pl.pallas_call(
        functools.partial(_decode_kernel, topk=topk, n_pages=np_, dec_seq=dm.dec_seq, pps=pps),
        grid_spec=pltpu.PrefetchScalarGridSpec(
            num_scalar_prefetch=1, grid=(db, 2 * ng),
            in_specs=[per_b(rows_hq, IDX_DIM), per_b(rows_hq, 1), per_b(N_KV_HEADS, rows_gq, HEAD_DIM),
                      per_b(PAGE_SIZE, IDX_DIM), per_b(PAGE_SIZE, KV_WIDTH), per_b(PAGE_SIZE, KV_WIDTH),
                      const(bfar.shape), const(bnear.shape), const(bcur.shape)]
                     + [ki_page(j) for j in pages] + [kv_page(j) for j in pages] + [kv_page(j) for j in pages],
            out_specs=per_b(N_KV_HEADS, rows_gq, HEAD_DIM),
            scratch_shapes=[pltpu.VMEM((np_ + 1, SUBLANES, LANES), I32), pltpu.VMEM((SUBLANES, LANES), I32),
                            pltpu.VMEM((N_KV_HEADS, rows_gq, 1), F32), pltpu.VMEM((N_KV_HEADS, rows_gq, 1), F32),
                            pltpu.VMEM((N_KV_HEADS, rows_gq, HEAD_DIM), F32)]),
        out_shape=jax.ShapeDtypeStruct((db, N_KV_HEADS, rows_gq, HEAD_DIM), F32),
        compiler_params=_params(("arbitrary", "arbitrary")),
        name="decode_attention",
    )(page_table, qi_s, w_col, q_s, ki_cur, k_cur, v_cur, bfar, bnear, bcur,
      *([cache_kidx] * pps), *([cache_k] * pps), *([cache_v] * pps))


CONV_CT = 256


def _conv_ln_swish(ext_ref, rows, w_ref, b_ref, g_ref, beta_ref, y_sc, out_ref_setter):
    off = CONV_HALO - (CONV_W - 1)
    for ct in range(CONV_CH // CONV_CT):
        cs = slice(ct * CONV_CT, (ct + 1) * CONV_CT)
        acc = jnp.zeros((rows, CONV_CT), F32)
        for j in range(CONV_W):
            acc = acc + ext_ref[pl.ds(off + j, rows), cs] * w_ref[j:j + 1, cs]
        y_sc[:, cs] = acc + b_ref[:, cs]
    y = y_sc[...]
    mu = jnp.mean(y, axis=-1, keepdims=True)
    d = y - mu
    var = jnp.mean(d * d, axis=-1, keepdims=True)
    yn = d * lax.rsqrt(var + EPS) * g_ref[...] + beta_ref[...]
    out_ref_setter(yn * jax.nn.sigmoid(yn))


def _prompt_conv_kernel(prev_ref, cur_ref, w_ref, b_ref, g_ref, beta_ref, c_ref, ext_sc, y_sc):
    i = pl.program_id(1)
    halo = prev_ref[QB - CONV_HALO:, :]
    ext_sc[:CONV_HALO, :] = jnp.where(i > 0, halo, 0.0)
    ext_sc[CONV_HALO:, :] = cur_ref[...]

    def put(c):
        c_ref[...] = c.astype(BF16)

    _conv_ln_swish(ext_sc, QB, w_ref, b_ref, g_ref, beta_ref, y_sc, put)


def _sample_conv_kernel(ext_ref, w_ref, b_ref, g_ref, beta_ref, c_ref, y_sc, *, rows):
    def put(c):
        c_ref[0] = c

    _conv_ln_swish(ext_ref.at[0], rows, w_ref, b_ref, g_ref, beta_ref, y_sc, put)


def _conv_weights(dw_w, dw_b, ln_g, ln_b):
    return (dw_w, dw_b.reshape(1, CONV_CH), ln_g.reshape(1, CONV_CH), ln_b.reshape(1, CONV_CH))


def _prompt_conv(dm, a_all, cw):
    nq = dm.nq
    blk = lambda f: pl.BlockSpec((QB, CONV_CH), f)
    return pl.pallas_call(
        _prompt_conv_kernel,
        grid=(dm.batch, nq),
        in_specs=[blk(lambda b, i: (jnp.maximum(b * nq + i - 1, 0), 0)), blk(lambda b, i: (b * nq + i, 0)),
                  _const_spec((CONV_W, CONV_CH))] + [_const_spec((1, CONV_CH))] * 3,
        out_specs=blk(lambda b, i: (b * nq + i, 0)),
        out_shape=jax.ShapeDtypeStruct((dm.prompt_rows, CONV_CH), BF16),
        scratch_shapes=[pltpu.VMEM((CONV_HALO + QB, CONV_CH), F32), pltpu.VMEM((QB, CONV_CH), F32)],
        compiler_params=_params(("arbitrary", "arbitrary")),
        name="prompt_conv",
    )(a_all, a_all, *cw)


def _sample_conv(dm, ext_s, cw):
    rows = dm.dec_seq
    tot = CONV_HALO + rows
    return pl.pallas_call(
        functools.partial(_sample_conv_kernel, rows=rows),
        grid=(dm.dec_batch,),
        in_specs=[pl.BlockSpec((1, tot, CONV_CH), lambda b: (b, 0, 0)), _const_spec((CONV_W, CONV_CH))]
                 + [_const_spec((1, CONV_CH))] * 3,
        out_specs=pl.BlockSpec((1, rows, CONV_CH), lambda b: (b, 0, 0)),
        out_shape=jax.ShapeDtypeStruct((dm.dec_batch, rows, CONV_CH), F32),
        scratch_shapes=[pltpu.VMEM((rows, CONV_CH), F32)],
        compiler_params=_params(("arbitrary",)),
        name="sample_conv",
    )(ext_s, *cw)


def _outproj_kernel(o_ref, c_ref, x_ref, wo_ref, wc_ref, g_ref, wpq_ref, h_ref, hn_ref, qp_ref):
    h = x_ref[...] + _dot(o_ref[...], wo_ref[...]) + _dot(c_ref[...], wc_ref[...])
    h_ref[...] = h
    ms = jnp.mean(h * h, axis=-1, keepdims=True)
    hn = h * lax.rsqrt(ms + EPS) * g_ref[...]
    hn_ref[...] = hn
    qp_ref[...] = _dot(hn.astype(BF16), wpq_ref[...]).astype(BF16)


def _out_projection(o_all, c_all, x_all, w_out, g_ffn, peer_wq):
    rows = x_all.shape[0]
    wo = w_out[:ATTN_WIDTH].astype(BF16)
    wc = w_out[ATTN_WIDTH:].astype(BF16)
    wpq = peer_wq.astype(BF16)
    row = lambda w: pl.BlockSpec((ROW_TILE, w), lambda i: (i, 0))
    pq = PEER_HEADS * PEER_KDIM
    return pl.pallas_call(
        _outproj_kernel,
        grid=(rows // ROW_TILE,),
        in_specs=[row(ATTN_WIDTH), row(CONV_CH), row(D_MODEL), _weight_spec(wo.shape), _weight_spec(wc.shape),
                  _const_spec((1, D_MODEL)), _weight_spec(wpq.shape)],
        out_specs=[row(D_MODEL), row(D_MODEL), row(pq)],
        out_shape=[jax.ShapeDtypeStruct((rows, D_MODEL), F32), jax.ShapeDtypeStruct((rows, D_MODEL), F32),
                   jax.ShapeDtypeStruct((rows, pq), BF16)],
        compiler_params=_params(("arbitrary",)),
        name="out_projection",
    )(o_all, c_all, x_all, wo, wc, g_ffn.reshape(1, D_MODEL), wpq)


def _extract_top(vals, pos, payload, count):
    big = jnp.int32(2 ** 30)
    out_v, out_p = [], []
    for _ in range(count):
        m = jnp.max(vals, axis=0, keepdims=True)
        first = jnp.min(jnp.where(vals == m, pos, big), axis=0, keepdims=True)
        hit = pos == first
        out_v.append(m)
        out_p.append(first if payload is None else jnp.sum(jnp.where(hit, payload, 0), axis=0, keepdims=True))
        vals = jnp.where(hit, -jnp.inf, vals)
    return jnp.concatenate(out_v, axis=0), jnp.concatenate(out_p, axis=0)


def _product_candidates(v1, i1, v2, i2):
    k = PEER_TOPK
    r8 = lax.broadcasted_iota(I32, (SUBLANES, LANES), 0)
    r16 = lax.broadcasted_iota(I32, (k, LANES), 0)
    vals = [v1[0:1] + v2]
    poss = [r16]
    idxs = [i1[0:1] * PEER_NKEYS + i2]
    for a in (1, 2, 3):
        vals.append(v1[a:a + 1] + v2[:SUBLANES])
        poss.append(a * k + r8)
        idxs.append(i1[a:a + 1] * PEER_NKEYS + i2[:SUBLANES])
    for b in (0, 1, 2):
        vals.append(jnp.where(r8 >= 4, v1[:SUBLANES] + v2[b:b + 1], -jnp.inf))
        poss.append(jnp.where(r8 >= 4, r8 * k + b, -1))
        idxs.append(i1[:SUBLANES] * PEER_NKEYS + i2[b:b + 1])
    vals.append(v1[SUBLANES:] + v2[0:1])
    poss.append((r8 + SUBLANES) * k)
    idxs.append(i1[SUBLANES:] * PEER_NKEYS + i2[0:1])
    cat = lambda xs: jnp.concatenate(xs, axis=0)
    return cat(vals), cat(poss), cat(idxs)


def _route_kernel(qp_ref, sk_ref, e_ref, g_ref):
    half = PEER_KDIM // 2
    key_id = lax.broadcasted_iota(I32, (PEER_NKEYS, LANES), 0)
    for h in range(PEER_HEADS):
        tops = []
        for c in range(2):
            qs = qp_ref[:, h * PEER_KDIM + c * half:h * PEER_KDIM + (c + 1) * half]
            s = _dot_nt(sk_ref[c], qs)
            tops.append(_extract_top(s, key_id, None, PEER_TOPK))
        (v1, i1), (v2, i2) = tops
        cand, cpos, cidx = _product_candidates(v1, i1, v2, i2)
        vals, experts = _extract_top(cand, cpos, cidx, PEER_TOPK)
        ex = jnp.exp(vals - vals[0:1, :])
        g_ref[h] = ex / jnp.sum(ex, axis=0, keepdims=True)
        e_ref[h] = experts


def _peer_route(qp, sub_keys):
    rows = qp.shape[0]
    sk = sub_keys.astype(BF16)
    out = pl.BlockSpec((PEER_HEADS, PEER_TOPK, LANES), lambda i: (0, 0, i))
    return pl.pallas_call(
        _route_kernel,
        grid=(rows // LANES,),
        in_specs=[pl.BlockSpec((LANES, PEER_HEADS * PEER_KDIM), lambda i: (i, 0)), _const_spec(sk.shape)],
        out_specs=[out, out],
        out_shape=[jax.ShapeDtypeStruct((PEER_HEADS, PEER_TOPK, rows), I32),
                   jax.ShapeDtypeStruct((PEER_HEADS, PEER_TOPK, rows), F32)],
        compiler_params=_params(("arbitrary",)),
        name="peer_route",
    )(qp, sk)


def _pack_bf16_pairs(t):
    bits = lax.bitcast_convert_type(t.astype(BF16), jnp.uint16).astype(jnp.uint32)
    half = t.shape[1] // 2
    return lax.bitcast_convert_type((bits[:, :half] << 16) | bits[:, half:], I32)


def _hi_half(w):
    return pltpu.bitcast(w & jnp.int32(-65536), F32)


def _lo_half(w):
    return pltpu.bitcast(lax.shift_left(w, jnp.int32(16)), F32)


def _peer_ffn_kernel(idx_ref, gate_ref, hn_ref, h_ref, uv_ref, y_ref, buf, out_sc, sem):
    n_groups = PEER_TOK_BLOCK // PEER_UNROLL
    half = D_MODEL // 2

    def issue(tok, slot):
        for k in range(PEER_SLOTS):
            pltpu.make_async_copy(uv_ref.at[idx_ref[tok, k]], buf.at[slot, k], sem.at[slot]).start()

    def wait(slot):
        pltpu.make_async_copy(uv_ref.at[pl.ds(0, PEER_SLOTS)], buf.at[slot], sem.at[slot]).wait()

    issue(0, 0)

    def group(gi, carry):
        t0 = pl.multiple_of(gi * PEER_UNROLL, PEER_UNROLL)
        gates_t = gate_ref[pl.ds(t0, PEER_UNROLL), :].T
        x_grp = hn_ref[pl.ds(t0, PEER_UNROLL), :]
        for j in range(PEER_UNROLL):
            slot = j % 2
            tok = t0 + j
            if j + 1 < PEER_UNROLL:
                issue(tok + 1, 1 - slot)
            else:
                @pl.when(gi + 1 < n_groups)
                def _():
                    issue(tok + 1, 1 - slot)
            wait(slot)
            hacc = jnp.zeros((PEER_SLOTS, LANES), F32)
            for c in range(half // LANES):
                lo_cols = slice(c * LANES, (c + 1) * LANES)
                hi_cols = slice(half + c * LANES, half + (c + 1) * LANES)
                w = buf[slot, :, lo_cols]
                hacc = hacc + _hi_half(w) * x_grp[j:j + 1, lo_cols] + _lo_half(w) * x_grp[j:j + 1, hi_cols]
            hcol = jnp.sum(hacc, axis=1, keepdims=True)
            act = gates_t[:, j:j + 1] * jax.nn.gelu(hcol)
            act_b = jnp.broadcast_to(act, (PEER_SLOTS, LANES))
            for c in range(half // LANES):
                lo_cols = slice(c * LANES, (c + 1) * LANES)
                hi_cols = slice(half + c * LANES, half + (c + 1) * LANES)
                w = buf[slot, :, half + c * LANES:half + (c + 1) * LANES]
                out_sc[j:j + 1, lo_cols] = jnp.sum(_hi_half(w) * act_b, axis=0, keepdims=True)
                out_sc[j:j + 1, hi_cols] = jnp.sum(_lo_half(w) * act_b, axis=0, keepdims=True)
        y_ref[pl.ds(t0, PEER_UNROLL), :] = h_ref[pl.ds(t0, PEER_UNROLL), :] + out_sc[...]
        return carry

    lax.fori_loop(0, n_groups, group, 0)


def _peer_ffn(experts_tok, gates_tok, hn2, h, uv):
    rows = h.shape[0]
    tb = PEER_TOK_BLOCK
    row = lambda w: pl.BlockSpec((tb, w), lambda i: (i, 0))
    return pl.pallas_call(
        _peer_ffn_kernel,
        grid=(rows // tb,),
        in_specs=[pl.BlockSpec((tb, PEER_SLOTS), lambda i: (i, 0), memory_space=pltpu.SMEM),
                  row(PEER_SLOTS), row(D_MODEL), row(D_MODEL), pl.BlockSpec(memory_space=pl.ANY)],
        out_specs=row(D_MODEL),
        out_shape=jax.ShapeDtypeStruct((rows, D_MODEL), F32),
        scratch_shapes=[pltpu.VMEM((2, PEER_SLOTS, D_MODEL), I32), pltpu.VMEM((PEER_UNROLL, D_MODEL), F32),
                        pltpu.SemaphoreType.DMA((2,))],
        compiler_params=pltpu.CompilerParams(dimension_semantics=("arbitrary",), vmem_limit_bytes=VMEM_LIMIT,
                                             disable_bounds_checks=True),
        name="peer_ffn",
    )(experts_tok, gates_tok, hn2, h, uv)


def _step(dm, x_prompt, x_sample, cache_k, cache_v, cache_kidx, state_conv, page_table, meta_tokens, rel_bias,
          g_attn, w_in, g_q, g_k, conv_dw_w, conv_dw_b, conv_ln_g, conv_ln_b, w_out, g_ffn, peer_wq, peer_subkeys,
          peer_u, peer_v):
    bsz, t, tp, db, ds = dm.batch, dm.t_prompt, dm.tp, dm.dec_batch, dm.dec_seq
    pr, sr = dm.prompt_rows, dm.sample_rows

    meta = jnp.broadcast_to(meta_tokens[None].astype(x_prompt.dtype), (bsz, N_META, D_MODEL))
    xp = jnp.concatenate([meta, x_prompt, jnp.zeros((bsz, tp - t, D_MODEL), x_prompt.dtype)], axis=1)
    x_all = jnp.concatenate([xp.reshape(pr, D_MODEL), x_sample.reshape(sr, D_MODEL),
                             jnp.zeros((dm.rows - pr - sr, D_MODEL), x_prompt.dtype)], axis=0)

    q_b, k_f, v_f, k_b, v_b, qi_b, kw, ki_b, a_all = _in_projection(x_all, g_attn[0], w_in[0], g_q[0], g_k[0])

    pb, dn, dc = _bias_tiles(rel_bias)
    far = rel_bias[N_BUCKETS - 1]
    cb = jnp.broadcast_to(far[:, None, None], (N_HEADS, 1, QB))
    o_p = _prompt_attention(dm, q_b, qi_b, kw, ki_b, k_b, v_b, pb, cb)

    s0, s1 = pr, pr + sr
    qi_s = qi_b[s0:s1].reshape(db, ds, IDX_HEADS, IDX_DIM).transpose(0, 2, 1, 3).reshape(db, IDX_HEADS * ds, IDX_DIM)
    w_col = kw[s0:s1, IDX_DIM:IDX_DIM + IDX_HEADS].reshape(db, ds, IDX_HEADS).transpose(0, 2, 1)
    w_col = w_col.reshape(db, IDX_HEADS * ds, 1)
    q_s = q_b[s0:s1].reshape(db, ds, N_KV_HEADS, GROUP, HEAD_DIM).transpose(0, 2, 3, 1, 4)
    q_s = q_s.reshape(db, N_KV_HEADS, GROUP * ds, HEAD_DIM)
    pad_keys = lambda z: jnp.pad(z.reshape(db, ds, -1), ((0, 0), (0, PAGE_SIZE - ds), (0, 0)))
    by_group = lambda z: z.reshape(N_KV_HEADS, GROUP * SUBLANES, LANES)
    bfar = jnp.broadcast_to(far.reshape(N_KV_HEADS, GROUP, 1, 1), (N_KV_HEADS, GROUP, SUBLANES, LANES))
    o_s = _decode_attention(
        dm, page_table, cache_k, cache_v, cache_kidx, qi_s, w_col, pad_keys(ki_b[s0:s1]), q_s, pad_keys(k_b[s0:s1]), pad_keys(v_b[s0:s1]),
        by_group(bfar), by_group(dn), by_group(dc))
    o_s = o_s.reshape(db, N_KV_HEADS, GROUP, ds, HEAD_DIM).transpose(0, 3, 1, 2, 4).reshape(sr, ATTN_WIDTH)

    cw = _conv_weights(conv_dw_w[0], conv_dw_b[0], conv_ln_g[0], conv_ln_b[0])
    c_p = _prompt_conv(dm, a_all, cw)
    a_s = a_all[s0:s1].reshape(db, ds, CONV_CH)
    ext_s = jnp.concatenate([jnp.zeros((db, CONV_HALO - (CONV_W - 1), CONV_CH), F32), state_conv[0], a_s], axis=1)
    c_s = _sample_conv(dm, ext_s, cw).reshape(sr, CONV_CH)

    tail = jnp.zeros((dm.rows - pr - sr, ATTN_WIDTH), BF16)
    o_all = jnp.concatenate([o_p, o_s.astype(BF16), tail], axis=0)
    c_all = jnp.concatenate([c_p, c_s.astype(BF16), tail], axis=0)
    h, hn2, qp = _out_projection(o_all, c_all, x_all, w_out[0], g_ffn[0], peer_wq[0])

    experts, gates = _peer_route(qp, peer_subkeys[0])
    experts_tok = experts.reshape(PEER_SLOTS, dm.rows).T
    gates_tok = gates.reshape(PEER_SLOTS, dm.rows).T
    uv = jnp.concatenate([_pack_bf16_pairs(peer_u[0]), _pack_bf16_pairs(peer_v[0])], axis=1)
    y = _peer_ffn(experts_tok, gates_tok, hn2, h, uv)

    per_seq = lambda z, w: z[:pr].reshape(bsz, tp, w)[:, :t]
    y_prompt = per_seq(y, D_MODEL)[:, N_META:]
    y_sample = y[s0:s1].reshape(db, ds, D_MODEL)
    kv_shape = (N_KV_HEADS, HEAD_DIM)
    new_k_p = per_seq(k_f, KV_WIDTH).reshape(1, bsz, t, *kv_shape)
    new_v_p = per_seq(v_f, KV_WIDTH).reshape(1, bsz, t, *kv_shape)
    new_ki_p = per_seq(kw, LANES)[:, :, :IDX_DIM][None]
    new_conv_p = per_seq(a_all, CONV_CH)[:, t - (CONV_W - 1):][None]
    new_k_s = k_f[s0:s1].reshape(1, db, ds, *kv_shape)
    new_v_s = v_f[s0:s1].reshape(1, db, ds, *kv_shape)
    new_ki_s = kw[s0:s1, :IDX_DIM].reshape(1, db, ds, IDX_DIM)
    new_conv_s = ext_s[:, -(CONV_W - 1):][None]
    return (y_prompt, y_sample, new_k_p, new_v_p, new_ki_p, new_conv_p, new_k_s, new_v_s, new_ki_s, new_conv_s)


def kernel(x_prompt, x_sample, cache_k, cache_v, cache_kidx, state_conv, page_table, meta_tokens, rel_bias, g_attn,
           w_in, g_q, g_k, conv_dw_w, conv_dw_b, conv_ln_g, conv_ln_b, w_out, g_ffn, peer_wq, peer_subkeys, peer_u,
           peer_v):
    assert g_attn.shape[0] == 1, "single trunk layer"
    dm = Dims(batch=x_prompt.shape[0], seq=x_prompt.shape[1], dec_batch=x_sample.shape[0],
              dec_seq=x_sample.shape[1], past_len=page_table.shape[1] * PAGE_SIZE)
    assert dm.dec_seq == SUBLANES and dm.t_prompt >= CONV_W - 1
    return _step(dm, x_prompt, x_sample, cache_k, cache_v, cache_kidx, state_conv, page_table, meta_tokens, rel_bias,
                 g_attn, w_in, g_q, g_k, conv_dw_w, conv_dw_b, conv_ln_g, conv_ln_b, w_out, g_ffn, peer_wq,
                 peer_subkeys, peer_u, peer_v)
```

```python
import functools
import math
from typing import NamedTuple

import jax
import jax.numpy as jnp
from jax import lax
from jax.experimental import pallas as pl
from jax.experimental.pallas import tpu as pltpu

F32 = jnp.float32
BF16 = jnp.bfloat16
I32 = jnp.int32

D_MODEL = 2048
N_META = 16
N_HEADS = 8
HEAD_DIM = 128
N_KV_HEADS = 2
GROUP = N_HEADS // N_KV_HEADS
ATTN_WIDTH = N_HEADS * HEAD_DIM
KV_WIDTH = N_KV_HEADS * HEAD_DIM
CONV_CH = D_MODEL - ATTN_WIDTH
CONV_W = 31
IDX_HEADS = 16
IDX_DIM = 64
IDX_SCALE = (IDX_HEADS * IDX_DIM) ** -0.5
TOPK_MAX = 256
N_BUCKETS = 32
MAX_DISTANCE = 128
ATTN_SCALE = HEAD_DIM ** -0.5
PAGE_SIZE = 128
PEER_HEADS = 8
PEER_NKEYS = 128
PEER_KDIM = 128
PEER_TOPK = 16
PEER_SLOTS = PEER_HEADS * PEER_TOPK
EPS = 1e-6

LANES = 128
SUBLANES = 8
QB = 128
ROW_TILE = 256
CONV_HALO = 32
PEER_TOK_BLOCK = 64
PEER_UNROLL = 8
PEER_BUFS = 4
PEER_AHEAD = PEER_BUFS - 1
PEER_DMA_QUEUES = 2
INT_MIN = -(2 ** 31)
NEG_BIG = -1e30
M_INIT = -1e29
VMEM_LIMIT = 56 * 1024 * 1024

NT_DIMS = (((1,), (1,)), ((), ()))
TN_DIMS = (((0,), (0,)), ((), ()))


class Dims(NamedTuple):
    batch: int
    seq: int
    dec_batch: int
    dec_seq: int
    past_len: int

    @property
    def t_prompt(self):
        return self.seq + N_META

    @property
    def nq(self):
        return -(-self.t_prompt // QB)

    @property
    def tp(self):
        return self.nq * QB

    @property
    def prompt_rows(self):
        return self.batch * self.tp

    @property
    def sample_rows(self):
        return self.dec_batch * self.dec_seq

    @property
    def rows(self):
        r = self.prompt_rows + self.sample_rows
        return -(-r // ROW_TILE) * ROW_TILE

    @property
    def n_pages(self):
        return self.past_len // PAGE_SIZE


def _dot(a, b):
    return jnp.dot(a, b, preferred_element_type=F32)


def _dot_nt(a, b):
    return lax.dot_general(a, b, NT_DIMS, preferred_element_type=F32)


def _dot_tn(a, b):
    return lax.dot_general(a, b, TN_DIMS, preferred_element_type=F32)


def _params(sem):
    return pltpu.CompilerParams(dimension_semantics=sem, vmem_limit_bytes=VMEM_LIMIT)


def _const_spec(shape):
    nd = len(shape)
    return pl.BlockSpec(shape, lambda *_: (0,) * nd)


def _weight_spec(shape):
    nd = len(shape)
    return pl.BlockSpec(shape, lambda *_: (0,) * nd, pipeline_mode=pl.Buffered(1))


def _order_key(x):
    bits = pltpu.bitcast(x + 0.0, I32)
    return bits ^ ((bits >> 31) & 0x7FFFFFFF)


def _inproj_kernel(x_ref, ga_ref, wq_ref, wkv_ref, wqi_ref, wkw_ref, wu_ref, gq_ref, gk_ref,
                   q_ref, kf_ref, vf_ref, kb_ref, vb_ref, qi_ref, kw_ref, kib_ref, a_ref):
    x = x_ref[...]
    ms = jnp.mean(x * x, axis=-1, keepdims=True)
    hn = (x * lax.rsqrt(ms + EPS) * ga_ref[...]).astype(BF16)

    def head_norm(z, g):
        zm = jnp.mean(z * z, axis=-1, keepdims=True)
        return z * lax.rsqrt(zm + EPS) * g

    q = _dot(hn, wq_ref[...])
    for h in range(N_HEADS):
        sl = slice(h * HEAD_DIM, (h + 1) * HEAD_DIM)
        q_ref[:, sl] = head_norm(q[:, sl], gq_ref[...]).astype(BF16)
    kv = _dot(hn, wkv_ref[...])
    for n in range(N_KV_HEADS):
        sl = slice(n * HEAD_DIM, (n + 1) * HEAD_DIM)
        kn = head_norm(kv[:, sl], gk_ref[...])
        kf_ref[:, sl] = kn
        kb_ref[:, sl] = kn.astype(BF16)
    v = kv[:, KV_WIDTH:]
    vf_ref[...] = v
    vb_ref[...] = v.astype(BF16)
    qi_ref[...] = _dot(hn, wqi_ref[...]).astype(BF16)
    kw = _dot(hn, wkw_ref[...])
    lane = lax.broadcasted_iota(I32, kw.shape, 1)
    kw = jnp.where(lane >= IDX_DIM, kw * IDX_SCALE, kw)
    kw_ref[...] = kw
    kib_ref[...] = kw[:, :IDX_DIM].astype(BF16)
    u = _dot(hn, wu_ref[...])
    a_ref[...] = u[:, :CONV_CH] * jax.nn.sigmoid(u[:, CONV_CH:])


def _in_projection(x_all, g_attn, w_in, g_q, g_k):
    rows = x_all.shape[0]
    o_q = ATTN_WIDTH
    o_k = o_q + KV_WIDTH
    o_v = o_k + KV_WIDTH
    o_qi = o_v + IDX_HEADS * IDX_DIM
    o_ki = o_qi + IDX_DIM
    o_wi = o_ki + IDX_HEADS
    wq = w_in[:, :o_q].astype(BF16)
    wkv = w_in[:, o_q:o_v].astype(BF16)
    wqi = w_in[:, o_v:o_qi].astype(BF16)
    wkw = jnp.pad(w_in[:, o_qi:o_wi], ((0, 0), (0, LANES - IDX_DIM - IDX_HEADS))).astype(BF16)
    wu = w_in[:, o_wi:].astype(BF16)
    row = lambda w: pl.BlockSpec((ROW_TILE, w), lambda i: (i, 0))
    outs = [
        (ATTN_WIDTH, BF16), (KV_WIDTH, F32), (KV_WIDTH, F32), (KV_WIDTH, BF16), (KV_WIDTH, BF16),
        (IDX_HEADS * IDX_DIM, BF16), (LANES, F32), (IDX_DIM, BF16), (CONV_CH, F32),
    ]
    return pl.pallas_call(
        _inproj_kernel,
        grid=(rows // ROW_TILE,),
        in_specs=[row(D_MODEL), _const_spec((1, D_MODEL)), _weight_spec(wq.shape), _weight_spec(wkv.shape),
                  _weight_spec(wqi.shape), _weight_spec(wkw.shape), _weight_spec(wu.shape),
                  _const_spec((1, HEAD_DIM)), _const_spec((1, HEAD_DIM))],
        out_specs=[row(w) for w, _ in outs],
        out_shape=[jax.ShapeDtypeStruct((rows, w), dt) for w, dt in outs],
        compiler_params=_params(("arbitrary",)),
        name="in_projection",
    )(x_all, g_attn.reshape(1, D_MODEL), wq, wkv, wqi, wkw, wu, g_q.reshape(1, HEAD_DIM), g_k.reshape(1, HEAD_DIM))


def _t5_bias_of(dist, rb_ref, h):
    n = jnp.maximum(dist, 0)
    max_exact = N_BUCKETS // 2
    nf = jnp.maximum(n, 1).astype(F32)
    large = max_exact + (jnp.log(nf / max_exact) / math.log(MAX_DISTANCE / max_exact)
                         * (N_BUCKETS - max_exact)).astype(I32)
    large = jnp.minimum(large, N_BUCKETS - 1)
    bucket = jnp.where(n < max_exact, n, large)
    out = jnp.zeros(dist.shape, F32)
    for k in range(N_BUCKETS):
        out = jnp.where(bucket == k, rb_ref[k, h], out)
    return out


def _bias_kernel(rb_ref, pb_ref, dn_ref, dc_ref):
    r2 = lax.broadcasted_iota(I32, (2 * QB, QB), 0)
    c2 = lax.broadcasted_iota(I32, (2 * QB, QB), 1)
    r1 = lax.broadcasted_iota(I32, (SUBLANES, LANES), 0)
    c1 = lax.broadcasted_iota(I32, (SUBLANES, LANES), 1)
    for h in range(N_HEADS):
        pb_ref[h] = _t5_bias_of(QB + c2 - r2, rb_ref, h)
        dn_ref[h] = _t5_bias_of(PAGE_SIZE + r1 - c1, rb_ref, h)
        dc_ref[h] = _t5_bias_of(r1 - c1, rb_ref, h)


def _bias_tiles(rel_bias):
    return pl.pallas_call(
        _bias_kernel,
        in_specs=[pl.BlockSpec(memory_space=pltpu.SMEM)],
        out_shape=[jax.ShapeDtypeStruct((N_HEADS, 2 * QB, QB), F32),
                   jax.ShapeDtypeStruct((N_HEADS, SUBLANES, LANES), F32),
                   jax.ShapeDtypeStruct((N_HEADS, SUBLANES, LANES), F32)],
        name="t5_bias_tiles",
    )(rel_bias)


def _prompt_attn_kernel(qi_ref, q_ref, kw_ref, ki_ref, k_ref, v_ref, pb_ref, cb_ref, o_ref,
                        key_sc, acc_sc, m_sc, l_sc, *, topk, lc):
    i = pl.program_id(1)
    q_lo = i * QB
    n_chunks = (q_lo + QB + lc - 1) // lc
    w_t = kw_ref[...].T
    row_l = lax.broadcasted_iota(I32, (lc, QB), 0)
    qpos_l = lax.broadcasted_iota(I32, (lc, QB), 1) + q_lo

    def score_chunk(c, carry):
        base = pl.multiple_of(c * lc, lc)
        kc = ki_ref[pl.ds(base, lc), :]
        acc = jnp.zeros((lc, QB), F32)
        for h in range(IDX_HEADS):
            s = _dot_nt(kc, qi_ref[:, h * IDX_DIM:(h + 1) * IDX_DIM])
            acc = acc + w_t[IDX_DIM + h:IDX_DIM + h + 1, :] * jnp.maximum(s, 0.0)
        key = jnp.where(row_l + base <= qpos_l, _order_key(acc), INT_MIN)
        key_sc[pl.ds(base, lc), :] = key
        return carry

    lax.fori_loop(0, n_chunks, score_chunk, 0)

    def count_keys(pred):
        def count_chunk(c, cnt):
            base = pl.multiple_of(c * lc, lc)
            hit = pred(key_sc[pl.ds(base, lc), :], row_l + base).astype(I32)
            return cnt + jnp.sum(hit.reshape(lc // SUBLANES, SUBLANES, QB), axis=0)

        cnt = lax.fori_loop(0, n_chunks, count_chunk, jnp.zeros((SUBLANES, QB), I32))
        return jnp.sum(cnt, axis=0, keepdims=True)

    def bisect(bit, thr):
        cand = thr + lax.shift_left(jnp.int32(1), 31 - bit)
        return jnp.where(count_keys(lambda k, l: k >= cand) >= topk, cand, thr)

    thr = lax.fori_loop(0, 32, bisect, jnp.full((1, QB), INT_MIN, I32))

    surplus = count_keys(lambda k, l: k >= thr) - topk
    tied = (surplus > 0) & (thr > INT_MIN)

    @pl.when(jnp.max(tied.astype(I32)) > 0)
    def _():
        need = topk - count_keys(lambda k, l: k > thr)
        pos_bits = key_sc.shape[0].bit_length()

        def bisect_pos(bit, cut):
            cand = cut + lax.shift_left(jnp.int32(1), pos_bits - 1 - bit)
            return jnp.where(count_keys(lambda k, l: (k == thr) & (l < cand)) < need, cand, cut)

        cut = lax.fori_loop(0, pos_bits, bisect_pos, jnp.zeros((1, QB), I32))

        def drop_chunk(c, carry):
            base = pl.multiple_of(c * lc, lc)
            k = key_sc[pl.ds(base, lc), :]
            drop = tied & (k == thr) & (row_l + base > cut)
            key_sc[pl.ds(base, lc), :] = jnp.where(drop, INT_MIN, k)
            return carry

        lax.fori_loop(0, n_chunks, drop_chunk, 0)

    thr = jnp.maximum(thr, INT_MIN + 1)

    m_sc[...] = jnp.full(m_sc.shape, M_INIT, F32)
    l_sc[...] = jnp.zeros(l_sc.shape, F32)
    acc_sc[...] = jnp.zeros(acc_sc.shape, F32)

    def attend(base, rows, sel, bias_of):
        kc = k_ref[pl.ds(base, rows), :]
        vc = v_ref[pl.ds(base, rows), :]
        for n in range(N_KV_HEADS):
            kn = kc[:, n * HEAD_DIM:(n + 1) * HEAD_DIM]
            vn = vc[:, n * HEAD_DIM:(n + 1) * HEAD_DIM]
            for g in range(GROUP):
                h = n * GROUP + g
                lg = _dot_nt(kn, q_ref[:, h * HEAD_DIM:(h + 1) * HEAD_DIM]) * ATTN_SCALE + bias_of(h)
                lg = jnp.where(sel, lg, NEG_BIG)
                m_old = m_sc[h:h + 1, :]
                m_new = jnp.maximum(m_old, jnp.max(lg, axis=0, keepdims=True))
                p = jnp.exp(lg - m_new)
                alpha = jnp.exp(m_old - m_new)
                l_sc[h:h + 1, :] = alpha * l_sc[h:h + 1, :] + jnp.sum(p, axis=0, keepdims=True)
                acc_sc[h] = alpha * acc_sc[h] + _dot_tn(vn, p.astype(BF16))
                m_sc[h:h + 1, :] = m_new

    far_hi = jnp.maximum(q_lo - QB, 0)
    n_far = (far_hi + lc - 1) // lc

    def far_chunk(c, carry):
        base = pl.multiple_of(c * lc, lc)
        sel = (key_sc[pl.ds(base, lc), :] >= thr) & (row_l + base < far_hi)
        attend(base, lc, sel, lambda h: cb_ref[h])
        return carry

    lax.fori_loop(0, n_far, far_chunk, 0)

    prev_lo = pl.multiple_of(jnp.maximum(q_lo - QB, 0), QB)
    sel_prev = (key_sc[pl.ds(prev_lo, QB), :] >= thr) & (i > 0)
    attend(prev_lo, QB, sel_prev, lambda h: pb_ref[h, :QB, :])
    diag_lo = pl.multiple_of(q_lo, QB)
    sel_diag = key_sc[pl.ds(diag_lo, QB), :] >= thr
    attend(diag_lo, QB, sel_diag, lambda h: pb_ref[h, QB:, :])

    for h in range(N_HEADS):
        o_t = acc_sc[h] / l_sc[h:h + 1, :]
        o_ref[:, h * HEAD_DIM:(h + 1) * HEAD_DIM] = o_t.T.astype(BF16)


def _prompt_attention(dm, q_b, qi_b, kw, ki_b, k_b, v_b, pb, cb):
    nq, tp = dm.nq, dm.tp
    topk = min(TOPK_MAX, dm.seq // 4)
    m = max(d for d in (1, 2, 3, 4) if nq % d == 0)
    lc = m * QB
    qblk = lambda w: pl.BlockSpec((QB, w), lambda b, i: (b * nq + i, 0))
    kblk = lambda w: pl.BlockSpec((tp, w), lambda b, i: (b, 0))
    return pl.pallas_call(
        functools.partial(_prompt_attn_kernel, topk=topk, lc=lc),
        grid=(dm.batch, nq),
        in_specs=[qblk(IDX_HEADS * IDX_DIM), qblk(ATTN_WIDTH), qblk(LANES),
                  kblk(IDX_DIM), kblk(KV_WIDTH), kblk(KV_WIDTH),
                  _const_spec(pb.shape), _const_spec(cb.shape)],
        out_specs=qblk(ATTN_WIDTH),
        out_shape=jax.ShapeDtypeStruct((dm.prompt_rows, ATTN_WIDTH), BF16),
        scratch_shapes=[pltpu.VMEM((tp, QB), I32), pltpu.VMEM((N_HEADS, HEAD_DIM, QB), F32),
                        pltpu.VMEM((N_HEADS, QB), F32), pltpu.VMEM((N_HEADS, QB), F32)],
        compiler_params=_params(("arbitrary", "arbitrary")),
        name="prompt_attention",
    )(qi_b, q_b, kw, ki_b, k_b, v_b, pb, cb)


DEC_PAGES_MAX = 16


def _decode_kernel(pt_ref, qi_ref, w_ref, q_ref, kicur_ref, kcur_ref, vcur_ref, bfar_ref, bnear_ref, bcur_ref, dup_ref,
                   *refs, topk, n_pages, dec_seq, pps):
    ki_refs = refs[:pps]
    k_refs = refs[pps:2 * pps]
    v_refs = refs[2 * pps:3 * pps]
    o_ref, key_sc, thr_sc, m_sc, l_sc, acc_sc = refs[3 * pps:]
    n_groups = n_pages // pps
    s = pl.program_id(1)
    n_rows = N_KV_HEADS * GROUP * SUBLANES
    n_cols = N_KV_HEADS * PAGE_SIZE

    def scores(keys_t_bf):
        sc = _dot(qi_ref[0], keys_t_bf)
        acc = jnp.zeros((SUBLANES, LANES), F32)
        for h in range(IDX_HEADS):
            rows = slice(h * SUBLANES, (h + 1) * SUBLANES)
            acc = acc + w_ref[0, rows, :] * jnp.maximum(sc[rows, :], 0.0)
        return _order_key(acc)

    @pl.when(s < n_groups)
    def _():
        for j in range(pps):
            key_sc[s * pps + j] = scores(ki_refs[j][...].astype(BF16))

    @pl.when(s == n_groups - 1)
    def _():
        r = lax.broadcasted_iota(I32, (SUBLANES, LANES), 0)
        c = lax.broadcasted_iota(I32, (SUBLANES, LANES), 1)
        key_sc[n_pages] = jnp.where((c <= r) & (c < dec_seq), scores(kicur_ref[0]), INT_MIN)

        def count_keys(pred):
            def count_page(j, cnt):
                return cnt + pred(key_sc[j], c + j * PAGE_SIZE).astype(I32)

            cnt = lax.fori_loop(0, n_pages + 1, count_page, jnp.zeros((SUBLANES, LANES), I32))
            return jnp.sum(cnt, axis=1, keepdims=True)

        def bisect(bit, thr):
            cand = thr + lax.shift_left(jnp.int32(1), 31 - bit)
            return jnp.where(count_keys(lambda k, l: k >= cand) >= topk, cand, thr)

        thr = lax.fori_loop(0, 32, bisect, jnp.full((SUBLANES, 1), INT_MIN, I32))

        surplus = count_keys(lambda k, l: k >= thr) - topk
        tied = (surplus > 0) & (thr > INT_MIN)

        @pl.when(jnp.max(tied.astype(I32)) > 0)
        def _():
            need = topk - count_keys(lambda k, l: k > thr)
            pos_bits = ((n_pages + 1) * PAGE_SIZE).bit_length()

            def bisect_pos(bit, cut):
                cand = cut + lax.shift_left(jnp.int32(1), pos_bits - 1 - bit)
                return jnp.where(count_keys(lambda k, l: (k == thr) & (l < cand)) < need, cand, cut)

            cut = lax.fori_loop(0, pos_bits, bisect_pos, jnp.zeros((SUBLANES, 1), I32))

            def drop_page(j, carry):
                k = key_sc[j]
                drop = tied & (k == thr) & (c + j * PAGE_SIZE > cut)
                key_sc[j] = jnp.where(drop, INT_MIN, k)
                return carry

            lax.fori_loop(0, n_pages + 1, drop_page, 0)

        thr_sc[...] = jnp.broadcast_to(jnp.maximum(thr, INT_MIN + 1), (SUBLANES, LANES))
        m_sc[...] = jnp.full(m_sc.shape, M_INIT, F32)
        l_sc[...] = jnp.zeros(l_sc.shape, F32)
        acc_sc[...] = jnp.zeros(acc_sc.shape, F32)

    def attend(key_tiles, k_of, v_of, bias_of):
        row_head = lax.broadcasted_iota(I32, (n_rows, n_cols), 0) // (GROUP * SUBLANES)
        col_head = lax.broadcasted_iota(I32, (n_rows, n_cols), 1) % N_KV_HEADS
        own_head = row_head == col_head
        thr = thr_sc[...]
        lgs = []
        for j, kt in enumerate(key_tiles):
            picked = jnp.where(kt >= thr, 1.0, 0.0).astype(BF16)
            picked = _dot(picked, dup_ref[...])
            picked = jnp.concatenate([picked] * (n_rows // SUBLANES), axis=0)
            lg = _dot_nt(q_ref[0], k_of(j)) * ATTN_SCALE + bias_of(j)
            lgs.append(jnp.where(jnp.where(own_head, picked, 0.0) > 0.5, lg, NEG_BIG))
        m_old = m_sc[...]
        m_new = jnp.maximum(m_old, jnp.max(functools.reduce(jnp.maximum, lgs), axis=1, keepdims=True))
        prs = [jnp.exp(lg - m_new) for lg in lgs]
        alpha = jnp.exp(m_old - m_new)
        l_sc[...] = alpha * l_sc[...] + jnp.sum(functools.reduce(jnp.add, prs), axis=1, keepdims=True)
        pv = functools.reduce(jnp.add, [_dot(pr.astype(BF16), v_of(j)) for j, pr in enumerate(prs)])
        acc_sc[...] = alpha * acc_sc[...] + pv
        m_sc[...] = m_new

    @pl.when(s >= n_groups)
    def _():
        g = s - n_groups
        last_group = g == n_groups - 1

        def bias_of(j):
            if j == pps - 1:
                return jnp.where(last_group, bnear_ref[...], bfar_ref[...])
            return bfar_ref[...]

        attend([key_sc[g * pps + j] for j in range(pps)],
               lambda j: k_refs[j][...].astype(BF16), lambda j: v_refs[j][...].astype(BF16), bias_of)

    @pl.when(s == 2 * n_groups - 1)
    def _():
        attend([key_sc[n_pages]], lambda j: kcur_ref[0], lambda j: vcur_ref[0], lambda j: bcur_ref[...])
        o_ref[0] = acc_sc[...] / l_sc[...]


def _decode_attention(dm, page_table, cache_k, cache_v, cache_kidx, qi_s, w_col, ki_cur, q_s, k_cur, v_cur,
                      bfar, bnear, bcur):
    db, np_ = dm.dec_batch, dm.n_pages
    topk = min(TOPK_MAX, (dm.past_len + dm.dec_seq) // 4)
    rows_hq = IDX_HEADS * SUBLANES
    n_rows = N_KV_HEADS * GROUP * SUBLANES
    n_cols = N_KV_HEADS * PAGE_SIZE
    pps = max(d for d in range(1, DEC_PAGES_MAX + 1) if np_ % d == 0)
    ng = np_ // pps
    ki_t = jnp.swapaxes(cache_kidx, 2, 3)
    k_rows = cache_k.reshape(cache_k.shape[:2] + (n_cols, HEAD_DIM))
    v_rows = cache_v.reshape(cache_v.shape[:2] + (n_cols, HEAD_DIM))
    dup = (jnp.arange(n_cols)[None, :] // N_KV_HEADS == jnp.arange(PAGE_SIZE)[:, None]).astype(BF16)
    per_b = lambda *shape: pl.BlockSpec((1,) + shape, lambda b, s, pt: (b,) + (0,) * len(shape))
    const = lambda shape: pl.BlockSpec(shape, lambda b, s, pt: (0,) * len(shape))
    ki_page = lambda j: pl.BlockSpec(
        (None, None, IDX_DIM, PAGE_SIZE), lambda b, s, pt: (0, pt[b, jnp.minimum(s, ng - 1) * pps + j], 0, 0))
    kv_page = lambda j: pl.BlockSpec(
        (None, None, n_cols, HEAD_DIM), lambda b, s, pt: (0, pt[b, jnp.maximum(s - ng, 0) * pps + j], 0, 0))
    pages = range(pps)
    return pl.pallas_call(
        functools.partial(_decode_kernel, topk=topk, n_pages=np_, dec_seq=dm.dec_seq, pps=pps),
        grid_spec=pltpu.PrefetchScalarGridSpec(
            num_scalar_prefetch=1, grid=(db, 2 * ng),
            in_specs=[per_b(rows_hq, IDX_DIM), per_b(rows_hq, 1), per_b(n_rows, HEAD_DIM),
                      per_b(IDX_DIM, PAGE_SIZE), per_b(n_cols, HEAD_DIM), per_b(n_cols, HEAD_DIM),
                      const(bfar.shape), const(bnear.shape), const(bcur.shape), const(dup.shape)]
                     + [ki_page(j) for j in pages] + [kv_page(j) for j in pages] + [kv_page(j) for j in pages],
            out_specs=per_b(n_rows, HEAD_DIM),
            scratch_shapes=[pltpu.VMEM((np_ + 1, SUBLANES, LANES), I32), pltpu.VMEM((SUBLANES, LANES), I32),
                            pltpu.VMEM((n_rows, 1), F32), pltpu.VMEM((n_rows, 1), F32),
                            pltpu.VMEM((n_rows, HEAD_DIM), F32)]),
        out_shape=jax.ShapeDtypeStruct((db, n_rows, HEAD_DIM), F32),
        compiler_params=_params(("arbitrary", "arbitrary")),
        name="decode_attention",
    )(page_table, qi_s, w_col, q_s, ki_cur, k_cur, v_cur, bfar, bnear, bcur, dup,
      *([ki_t] * pps), *([k_rows] * pps), *([v_rows] * pps))


CONV_CT = 256


def _conv_ln_swish(ext_ref, rows, w_ref, b_ref, g_ref, beta_ref, y_sc, out_ref_setter):
    off = CONV_HALO - (CONV_W - 1)
    for ct in range(CONV_CH // CONV_CT):
        cs = slice(ct * CONV_CT, (ct + 1) * CONV_CT)
        acc = jnp.zeros((rows, CONV_CT), F32)
        for j in range(CONV_W):
            acc = acc + ext_ref[pl.ds(off + j, rows), cs] * w_ref[j:j + 1, cs]
        y_sc[:, cs] = acc + b_ref[:, cs]
    y = y_sc[...]
    mu = jnp.mean(y, axis=-1, keepdims=True)
    d = y - mu
    var = jnp.mean(d * d, axis=-1, keepdims=True)
    yn = d * lax.rsqrt(var + EPS) * g_ref[...] + beta_ref[...]
    out_ref_setter(yn * jax.nn.sigmoid(yn))


def _prompt_conv_kernel(prev_ref, cur_ref, w_ref, b_ref, g_ref, beta_ref, c_ref, ext_sc, y_sc):
    i = pl.program_id(1)
    halo = prev_ref[QB - CONV_HALO:, :]
    ext_sc[:CONV_HALO, :] = jnp.where(i > 0, halo, 0.0)
    ext_sc[CONV_HALO:, :] = cur_ref[...]

    def put(c):
        c_ref[...] = c.astype(BF16)

    _conv_ln_swish(ext_sc, QB, w_ref, b_ref, g_ref, beta_ref, y_sc, put)


def _sample_conv_kernel(ext_ref, w_ref, b_ref, g_ref, beta_ref, c_ref, y_sc, *, rows):
    def put(c):
        c_ref[0] = c

    _conv_ln_swish(ext_ref.at[0], rows, w_ref, b_ref, g_ref, beta_ref, y_sc, put)


def _conv_weights(dw_w, dw_b, ln_g, ln_b):
    return (dw_w, dw_b.reshape(1, CONV_CH), ln_g.reshape(1, CONV_CH), ln_b.reshape(1, CONV_CH))


def _prompt_conv(dm, a_all, cw):
    nq = dm.nq
    blk = lambda f: pl.BlockSpec((QB, CONV_CH), f)
    return pl.pallas_call(
        _prompt_conv_kernel,
        grid=(dm.batch, nq),
        in_specs=[blk(lambda b, i: (jnp.maximum(b * nq + i - 1, 0), 0)), blk(lambda b, i: (b * nq + i, 0)),
                  _const_spec((CONV_W, CONV_CH))] + [_const_spec((1, CONV_CH))] * 3,
        out_specs=blk(lambda b, i: (b * nq + i, 0)),
        out_shape=jax.ShapeDtypeStruct((dm.prompt_rows, CONV_CH), BF16),
        scratch_shapes=[pltpu.VMEM((CONV_HALO + QB, CONV_CH), F32), pltpu.VMEM((QB, CONV_CH), F32)],
        compiler_params=_params(("arbitrary", "arbitrary")),
        name="prompt_conv",
    )(a_all, a_all, *cw)


def _sample_conv(dm, ext_s, cw):
    rows = dm.dec_seq
    tot = CONV_HALO + rows
    return pl.pallas_call(
        functools.partial(_sample_conv_kernel, rows=rows),
        grid=(dm.dec_batch,),
        in_specs=[pl.BlockSpec((1, tot, CONV_CH), lambda b: (b, 0, 0)), _const_spec((CONV_W, CONV_CH))]
                 + [_const_spec((1, CONV_CH))] * 3,
        out_specs=pl.BlockSpec((1, rows, CONV_CH), lambda b: (b, 0, 0)),
        out_shape=jax.ShapeDtypeStruct((dm.dec_batch, rows, CONV_CH), F32),
        scratch_shapes=[pltpu.VMEM((rows, CONV_CH), F32)],
        compiler_params=_params(("arbitrary",)),
        name="sample_conv",
    )(ext_s, *cw)


def _outproj_kernel(o_ref, c_ref, x_ref, wo_ref, wc_ref, g_ref, wpq_ref, h_ref, hn_ref, qp_ref):
    h = x_ref[...] + _dot(o_ref[...], wo_ref[...]) + _dot(c_ref[...], wc_ref[...])
    h_ref[...] = h
    ms = jnp.mean(h * h, axis=-1, keepdims=True)
    hn = h * lax.rsqrt(ms + EPS) * g_ref[...]
    hn_ref[...] = hn
    qp_ref[...] = _dot(hn.astype(BF16), wpq_ref[...]).astype(BF16)


def _out_projection(o_all, c_all, x_all, w_out, g_ffn, peer_wq):
    rows = x_all.shape[0]
    wo = w_out[:ATTN_WIDTH].astype(BF16)
    wc = w_out[ATTN_WIDTH:].astype(BF16)
    wpq = peer_wq.astype(BF16)
    row = lambda w: pl.BlockSpec((ROW_TILE, w), lambda i: (i, 0))
    pq = PEER_HEADS * PEER_KDIM
    return pl.pallas_call(
        _outproj_kernel,
        grid=(rows // ROW_TILE,),
        in_specs=[row(ATTN_WIDTH), row(CONV_CH), row(D_MODEL), _weight_spec(wo.shape), _weight_spec(wc.shape),
                  _const_spec((1, D_MODEL)), _weight_spec(wpq.shape)],
        out_specs=[row(D_MODEL), row(D_MODEL), row(pq)],
        out_shape=[jax.ShapeDtypeStruct((rows, D_MODEL), F32), jax.ShapeDtypeStruct((rows, D_MODEL), F32),
                   jax.ShapeDtypeStruct((rows, pq), BF16)],
        compiler_params=_params(("arbitrary",)),
        name="out_projection",
    )(o_all, c_all, x_all, wo, wc, g_ffn.reshape(1, D_MODEL), wpq)


def _extract_top(vals, pos, payload, count):
    big = jnp.int32(2 ** 30)
    out_v, out_p = [], []
    for _ in range(count):
        m = jnp.max(vals, axis=0, keepdims=True)
        first = jnp.min(jnp.where(vals == m, pos, big), axis=0, keepdims=True)
        hit = pos == first
        out_v.append(m)
        out_p.append(first if payload is None else jnp.sum(jnp.where(hit, payload, 0), axis=0, keepdims=True))
        vals = jnp.where(hit, -jnp.inf, vals)
    return jnp.concatenate(out_v, axis=0), jnp.concatenate(out_p, axis=0)


def _product_candidates(v1, i1, v2, i2):
    k = PEER_TOPK
    r8 = lax.broadcasted_iota(I32, (SUBLANES, LANES), 0)
    r16 = lax.broadcasted_iota(I32, (k, LANES), 0)
    vals = [v1[0:1] + v2]
    poss = [r16]
    idxs = [i1[0:1] * PEER_NKEYS + i2]
    for a in (1, 2, 3):
        vals.append(v1[a:a + 1] + v2[:SUBLANES])
        poss.append(a * k + r8)
        idxs.append(i1[a:a + 1] * PEER_NKEYS + i2[:SUBLANES])
    for b in (0, 1, 2):
        vals.append(jnp.where(r8 >= 4, v1[:SUBLANES] + v2[b:b + 1], -jnp.inf))
        poss.append(jnp.where(r8 >= 4, r8 * k + b, -1))
        idxs.append(i1[:SUBLANES] * PEER_NKEYS + i2[b:b + 1])
    vals.append(v1[SUBLANES:] + v2[0:1])
    poss.append((r8 + SUBLANES) * k)
    idxs.append(i1[SUBLANES:] * PEER_NKEYS + i2[0:1])
    cat = lambda xs: jnp.concatenate(xs, axis=0)
    return cat(vals), cat(poss), cat(idxs)


def _route_kernel(qp_ref, sk_ref, e_ref, g_ref):
    half = PEER_KDIM // 2
    key_id = lax.broadcasted_iota(I32, (PEER_NKEYS, LANES), 0)
    for h in range(PEER_HEADS):
        tops = []
        for c in range(2):
            qs = qp_ref[:, h * PEER_KDIM + c * half:h * PEER_KDIM + (c + 1) * half]
            s = _dot_nt(sk_ref[c], qs)
            tops.append(_extract_top(s, key_id, None, PEER_TOPK))
        (v1, i1), (v2, i2) = tops
        cand, cpos, cidx = _product_candidates(v1, i1, v2, i2)
        vals, experts = _extract_top(cand, cpos, cidx, PEER_TOPK)
        ex = jnp.exp(vals - vals[0:1, :])
        g_ref[h] = ex / jnp.sum(ex, axis=0, keepdims=True)
        e_ref[h] = experts


def _peer_route(qp, sub_keys):
    rows = qp.shape[0]
    sk = sub_keys.astype(BF16)
    out = pl.BlockSpec((PEER_HEADS, PEER_TOPK, LANES), lambda i: (0, 0, i))
    return pl.pallas_call(
        _route_kernel,
        grid=(rows // LANES,),
        in_specs=[pl.BlockSpec((LANES, PEER_HEADS * PEER_KDIM), lambda i: (i, 0)), _const_spec(sk.shape)],
        out_specs=[out, out],
        out_shape=[jax.ShapeDtypeStruct((PEER_HEADS, PEER_TOPK, rows), I32),
                   jax.ShapeDtypeStruct((PEER_HEADS, PEER_TOPK, rows), F32)],
        compiler_params=_params(("arbitrary",)),
        name="peer_route",
    )(qp, sk)


def _pack_bf16_pairs(t):
    bits = lax.bitcast_convert_type(t.astype(BF16), jnp.uint16).astype(jnp.uint32)
    half = t.shape[1] // 2
    return lax.bitcast_convert_type((bits[:, :half] << 16) | bits[:, half:], I32)


def _hi_half(w):
    return pltpu.bitcast(w & jnp.int32(-65536), F32)


def _lo_half(w):
    return pltpu.bitcast(lax.shift_left(w, jnp.int32(16)), F32)


def _peer_ffn_kernel(idx_ref, gate_ref, hn_ref, h_ref, uv_ref, y_ref, buf, out_sc, sem):
    n_groups = PEER_TOK_BLOCK // PEER_UNROLL
    half = D_MODEL // 2

    def issue(tok, slot):
        for k in range(PEER_SLOTS):
            pltpu.make_async_copy(uv_ref.at[idx_ref[tok, k]], buf.at[slot, k], sem.at[slot]).start(
                priority=k % PEER_DMA_QUEUES)

    def wait(slot):
        pltpu.make_async_copy(uv_ref.at[pl.ds(0, PEER_SLOTS)], buf.at[slot], sem.at[slot]).wait()

    for j in range(PEER_AHEAD):
        issue(j, j % PEER_BUFS)

    def group(gi, carry):
        t0 = pl.multiple_of(gi * PEER_UNROLL, PEER_UNROLL)
        gates_t = gate_ref[pl.ds(t0, PEER_UNROLL), :].T
        x_grp = hn_ref[pl.ds(t0, PEER_UNROLL), :]
        for j in range(PEER_UNROLL):
            slot = j % PEER_BUFS
            ahead = j + PEER_AHEAD
            if ahead < PEER_UNROLL:
                issue(t0 + ahead, ahead % PEER_BUFS)
            else:
                @pl.when(gi + 1 < n_groups)
                def _():
                    issue(t0 + ahead, ahead % PEER_BUFS)
            wait(slot)
            hacc = jnp.zeros((PEER_SLOTS, LANES), F32)
            for c in range(half // LANES):
                lo_cols = slice(c * LANES, (c + 1) * LANES)
                hi_cols = slice(half + c * LANES, half + (c + 1) * LANES)
                w = buf[slot, :, lo_cols]
                hacc = hacc + _hi_half(w) * x_grp[j:j + 1, lo_cols] + _lo_half(w) * x_grp[j:j + 1, hi_cols]
            hcol = jnp.sum(hacc, axis=1, keepdims=True)
            act = gates_t[:, j:j + 1] * jax.nn.gelu(hcol)
            act_b = jnp.broadcast_to(act, (PEER_SLOTS, LANES))
            for c in range(half // LANES):
                lo_cols = slice(c * LANES, (c + 1) * LANES)
                hi_cols = slice(half + c * LANES, half + (c + 1) * LANES)
                w = buf[slot, :, half + c * LANES:half + (c + 1) * LANES]
                out_sc[j:j + 1, lo_cols] = jnp.sum(_hi_half(w) * act_b, axis=0, keepdims=True)
                out_sc[j:j + 1, hi_cols] = jnp.sum(_lo_half(w) * act_b, axis=0, keepdims=True)
        y_ref[pl.ds(t0, PEER_UNROLL), :] = h_ref[pl.ds(t0, PEER_UNROLL), :] + out_sc[...]
        return carry

    lax.fori_loop(0, n_groups, group, 0)


def _peer_ffn(experts_tok, gates_tok, hn2, h, uv):
    rows = h.shape[0]
    tb = PEER_TOK_BLOCK
    row = lambda w: pl.BlockSpec((tb, w), lambda i: (i, 0))
    return pl.pallas_call(
        _peer_ffn_kernel,
        grid=(rows // tb,),
        in_specs=[pl.BlockSpec((tb, PEER_SLOTS), lambda i: (i, 0), memory_space=pltpu.SMEM),
                  row(PEER_SLOTS), row(D_MODEL), row(D_MODEL), pl.BlockSpec(memory_space=pl.ANY)],
        out_specs=row(D_MODEL),
        out_shape=jax.ShapeDtypeStruct((rows, D_MODEL), F32),
        scratch_shapes=[pltpu.VMEM((PEER_BUFS, PEER_SLOTS, D_MODEL), I32), pltpu.VMEM((PEER_UNROLL, D_MODEL), F32),
                        pltpu.SemaphoreType.DMA((PEER_BUFS,))],
        compiler_params=pltpu.CompilerParams(dimension_semantics=("arbitrary",), vmem_limit_bytes=VMEM_LIMIT,
                                             disable_bounds_checks=True),
        name="peer_ffn",
    )(experts_tok, gates_tok, hn2, h, uv)


def _step(dm, x_prompt, x_sample, cache_k, cache_v, cache_kidx, state_conv, page_table, meta_tokens, rel_bias,
          g_attn, w_in, g_q, g_k, conv_dw_w, conv_dw_b, conv_ln_g, conv_ln_b, w_out, g_ffn, peer_wq, peer_subkeys,
          peer_u, peer_v):
    bsz, t, tp, db, ds = dm.batch, dm.t_prompt, dm.tp, dm.dec_batch, dm.dec_seq
    pr, sr = dm.prompt_rows, dm.sample_rows

    meta = jnp.broadcast_to(meta_tokens[None].astype(x_prompt.dtype), (bsz, N_META, D_MODEL))
    xp = jnp.concatenate([meta, x_prompt, jnp.zeros((bsz, tp - t, D_MODEL), x_prompt.dtype)], axis=1)
    x_all = jnp.concatenate([xp.reshape(pr, D_MODEL), x_sample.reshape(sr, D_MODEL),
                             jnp.zeros((dm.rows - pr - sr, D_MODEL), x_prompt.dtype)], axis=0)

    q_b, k_f, v_f, k_b, v_b, qi_b, kw, ki_b, a_all = _in_projection(x_all, g_attn[0], w_in[0], g_q[0], g_k[0])

    pb, dn, dc = _bias_tiles(rel_bias)
    far = rel_bias[N_BUCKETS - 1]
    cb = jnp.broadcast_to(far[:, None, None], (N_HEADS, 1, QB))
    o_p = _prompt_attention(dm, q_b, qi_b, kw, ki_b, k_b, v_b, pb, cb)

    s0, s1 = pr, pr + sr
    qi_s = qi_b[s0:s1].reshape(db, ds, IDX_HEADS, IDX_DIM).transpose(0, 2, 1, 3).reshape(db, IDX_HEADS * ds, IDX_DIM)
    w_col = kw[s0:s1, IDX_DIM:IDX_DIM + IDX_HEADS].reshape(db, ds, IDX_HEADS).transpose(0, 2, 1)
    w_col = w_col.reshape(db, IDX_HEADS * ds, 1)
    q_s = q_b[s0:s1].reshape(db, ds, N_KV_HEADS, GROUP, HEAD_DIM).transpose(0, 2, 3, 1, 4)
    q_s = q_s.reshape(db, N_HEADS * ds, HEAD_DIM)
    pad_keys = lambda z: jnp.pad(z.reshape(db, ds, -1), ((0, 0), (0, PAGE_SIZE - ds), (0, 0)))
    kv_rows = lambda z: pad_keys(z).reshape(db, PAGE_SIZE * N_KV_HEADS, HEAD_DIM)
    by_rows = lambda z: jnp.repeat(z.reshape(N_HEADS * SUBLANES, LANES), N_KV_HEADS, axis=1)
    bfar = jnp.broadcast_to(far[:, None, None], (N_HEADS, SUBLANES, LANES))
    o_s = _decode_attention(
        dm, page_table, cache_k, cache_v, cache_kidx, qi_s, w_col, jnp.swapaxes(pad_keys(ki_b[s0:s1]), 1, 2), q_s,
        kv_rows(k_b[s0:s1]), kv_rows(v_b[s0:s1]), by_rows(bfar), by_rows(dn), by_rows(dc))
    o_s = o_s.reshape(db, N_HEADS, ds, HEAD_DIM).transpose(0, 2, 1, 3).reshape(sr, ATTN_WIDTH)

    cw = _conv_weights(conv_dw_w[0], conv_dw_b[0], conv_ln_g[0], conv_ln_b[0])
    c_p = _prompt_conv(dm, a_all, cw)
    a_s = a_all[s0:s1].reshape(db, ds, CONV_CH)
    ext_s = jnp.concatenate([jnp.zeros((db, CONV_HALO - (CONV_W - 1), CONV_CH), F32), state_conv[0], a_s], axis=1)
    c_s = _sample_conv(dm, ext_s, cw).reshape(sr, CONV_CH)

    tail = jnp.zeros((dm.rows - pr - sr, ATTN_WIDTH), BF16)
    o_all = jnp.concatenate([o_p, o_s.astype(BF16), tail], axis=0)
    c_all = jnp.concatenate([c_p, c_s.astype(BF16), tail], axis=0)
    h, hn2, qp = _out_projection(o_all, c_all, x_all, w_out[0], g_ffn[0], peer_wq[0])

    experts, gates = _peer_route(qp, peer_subkeys[0])
    experts_tok = experts.reshape(PEER_SLOTS, dm.rows).T
    gates_tok = gates.reshape(PEER_SLOTS, dm.rows).T
    uv = jnp.concatenate([_pack_bf16_pairs(peer_u[0]), _pack_bf16_pairs(peer_v[0])], axis=1)
    y = _peer_ffn(experts_tok, gates_tok, hn2, h, uv)

    per_seq = lambda z, w: z[:pr].reshape(bsz, tp, w)[:, :t]
    y_prompt = per_seq(y, D_MODEL)[:, N_META:]
    y_sample = y[s0:s1].reshape(db, ds, D_MODEL)
    kv_shape = (N_KV_HEADS, HEAD_DIM)
    new_k_p = per_seq(k_f, KV_WIDTH).reshape(1, bsz, t, *kv_shape)
    new_v_p = per_seq(v_f, KV_WIDTH).reshape(1, bsz, t, *kv_shape)
    new_ki_p = per_seq(kw, LANES)[:, :, :IDX_DIM][None]
    new_conv_p = per_seq(a_all, CONV_CH)[:, t - (CONV_W - 1):][None]
    new_k_s = k_f[s0:s1].reshape(1, db, ds, *kv_shape)
    new_v_s = v_f[s0:s1].reshape(1, db, ds, *kv_shape)
    new_ki_s = kw[s0:s1, :IDX_DIM].reshape(1, db, ds, IDX_DIM)
    new_conv_s = ext_s[:, -(CONV_W - 1):][None]
    return (y_prompt, y_sample, new_k_p, new_v_p, new_ki_p, new_conv_p, new_k_s, new_v_s, new_ki_s, new_conv_s)


def kernel(x_prompt, x_sample, cache_k, cache_v, cache_kidx, state_conv, page_table, meta_tokens, rel_bias, g_attn,
           w_in, g_q, g_k, conv_dw_w, conv_dw_b, conv_ln_g, conv_ln_b, w_out, g_ffn, peer_wq, peer_subkeys, peer_u,
           peer_v):
    assert g_attn.shape[0] == 1, "single trunk layer"
    dm = Dims(batch=x_prompt.shape[0], seq=x_prompt.shape[1], dec_batch=x_sample.shape[0],
              dec_seq=x_sample.shape[1], past_len=page_table.shape[1] * PAGE_SIZE)
    assert dm.dec_seq == SUBLANES and dm.t_prompt >= CONV_W - 1
    return _step(dm, x_prompt, x_sample, cache_k, cache_v, cache_kidx, state_conv, page_table, meta_tokens, rel_bias,
                 g_attn, w_in, g_q, g_k, conv_dw_w, conv_dw_b, conv_ln_g, conv_ln_b, w_out, g_ffn, peer_wq,
                 peer_subkeys, peer_u, peer_v)
```

```python
import functools
import math
from typing import NamedTuple

import jax
import jax.numpy as jnp
from jax import lax
from jax.experimental import pallas as pl
from jax.experimental.pallas import tpu as pltpu

F32 = jnp.float32
BF16 = jnp.bfloat16
I32 = jnp.int32

D_MODEL = 2048
N_META = 16
N_HEADS = 8
HEAD_DIM = 128
N_KV_HEADS = 2
GROUP = N_HEADS // N_KV_HEADS
ATTN_WIDTH = N_HEADS * HEAD_DIM
KV_WIDTH = N_KV_HEADS * HEAD_DIM
CONV_CH = D_MODEL - ATTN_WIDTH
CONV_W = 31
IDX_HEADS = 16
IDX_DIM = 64
IDX_SCALE = (IDX_HEADS * IDX_DIM) ** -0.5
TOPK_MAX = 256
N_BUCKETS = 32
MAX_DISTANCE = 128
ATTN_SCALE = HEAD_DIM ** -0.5
PAGE_SIZE = 128
PEER_HEADS = 8
PEER_NKEYS = 128
PEER_KDIM = 128
PEER_TOPK = 16
PEER_SLOTS = PEER_HEADS * PEER_TOPK
EPS = 1e-6

LANES = 128
SUBLANES = 8
QB = 128
ROW_TILE = 256
CONV_HALO = 32
PEER_TOK_BLOCK = 64
PEER_UNROLL = 8
PEER_BUFS = 4
PEER_AHEAD = PEER_BUFS - 1
PEER_DMA_QUEUES = 2
INT_MIN = -(2 ** 31)
NEG_BIG = -1e30
M_INIT = -1e29
VMEM_LIMIT = 56 * 1024 * 1024

NT_DIMS = (((1,), (1,)), ((), ()))
TN_DIMS = (((0,), (0,)), ((), ()))


class Dims(NamedTuple):
    batch: int
    seq: int
    dec_batch: int
    dec_seq: int
    past_len: int

    @property
    def t_prompt(self):
        return self.seq + N_META

    @property
    def nq(self):
        return -(-self.t_prompt // QB)

    @property
    def tp(self):
        return self.nq * QB

    @property
    def prompt_rows(self):
        return self.batch * self.tp

    @property
    def sample_rows(self):
        return self.dec_batch * self.dec_seq

    @property
    def rows(self):
        r = self.prompt_rows + self.sample_rows
        return -(-r // ROW_TILE) * ROW_TILE

    @property
    def n_pages(self):
        return self.past_len // PAGE_SIZE


def _dot(a, b):
    return jnp.dot(a, b, preferred_element_type=F32)


def _dot_nt(a, b):
    return lax.dot_general(a, b, NT_DIMS, preferred_element_type=F32)


def _dot_tn(a, b):
    return lax.dot_general(a, b, TN_DIMS, preferred_element_type=F32)


def _params(sem):
    return pltpu.CompilerParams(dimension_semantics=sem, vmem_limit_bytes=VMEM_LIMIT)


def _const_spec(shape):
    nd = len(shape)
    return pl.BlockSpec(shape, lambda *_: (0,) * nd)


def _weight_spec(shape):
    nd = len(shape)
    return pl.BlockSpec(shape, lambda *_: (0,) * nd, pipeline_mode=pl.Buffered(1))


def _order_key(x):
    bits = pltpu.bitcast(x + 0.0, I32)
    return bits ^ ((bits >> 31) & 0x7FFFFFFF)


def _inproj_kernel(x_ref, ga_ref, wq_ref, wkv_ref, wqi_ref, wkw_ref, wu_ref, gq_ref, gk_ref,
                   q_ref, kf_ref, vf_ref, kb_ref, vb_ref, qi_ref, kw_ref, kib_ref, a_ref):
    x = x_ref[...]
    ms = jnp.mean(x * x, axis=-1, keepdims=True)
    hn = (x * lax.rsqrt(ms + EPS) * ga_ref[...]).astype(BF16)

    def head_norm(z, g):
        zm = jnp.mean(z * z, axis=-1, keepdims=True)
        return z * lax.rsqrt(zm + EPS) * g

    q = _dot(hn, wq_ref[...])
    for h in range(N_HEADS):
        sl = slice(h * HEAD_DIM, (h + 1) * HEAD_DIM)
        q_ref[:, sl] = head_norm(q[:, sl], gq_ref[...]).astype(BF16)
    kv = _dot(hn, wkv_ref[...])
    for n in range(N_KV_HEADS):
        sl = slice(n * HEAD_DIM, (n + 1) * HEAD_DIM)
        kn = head_norm(kv[:, sl], gk_ref[...])
        kf_ref[:, sl] = kn
        kb_ref[:, sl] = kn.astype(BF16)
    v = kv[:, KV_WIDTH:]
    vf_ref[...] = v
    vb_ref[...] = v.astype(BF16)
    qi_ref[...] = _dot(hn, wqi_ref[...]).astype(BF16)
    kw = _dot(hn, wkw_ref[...])
    lane = lax.broadcasted_iota(I32, kw.shape, 1)
    kw = jnp.where(lane >= IDX_DIM, kw * IDX_SCALE, kw)
    kw_ref[...] = kw
    kib_ref[...] = kw[:, :IDX_DIM].astype(BF16)
    u = _dot(hn, wu_ref[...])
    a_ref[...] = u[:, :CONV_CH] * jax.nn.sigmoid(u[:, CONV_CH:])


def _in_projection(x_all, g_attn, w_in, g_q, g_k):
    rows = x_all.shape[0]
    o_q = ATTN_WIDTH
    o_k = o_q + KV_WIDTH
    o_v = o_k + KV_WIDTH
    o_qi = o_v + IDX_HEADS * IDX_DIM
    o_ki = o_qi + IDX_DIM
    o_wi = o_ki + IDX_HEADS
    wq = w_in[:, :o_q].astype(BF16)
    wkv = w_in[:, o_q:o_v].astype(BF16)
    wqi = w_in[:, o_v:o_qi].astype(BF16)
    wkw = jnp.pad(w_in[:, o_qi:o_wi], ((0, 0), (0, LANES - IDX_DIM - IDX_HEADS))).astype(BF16)
    wu = w_in[:, o_wi:].astype(BF16)
    row = lambda w: pl.BlockSpec((ROW_TILE, w), lambda i: (i, 0))
    outs = [
        (ATTN_WIDTH, BF16), (KV_WIDTH, F32), (KV_WIDTH, F32), (KV_WIDTH, BF16), (KV_WIDTH, BF16),
        (IDX_HEADS * IDX_DIM, BF16), (LANES, F32), (IDX_DIM, BF16), (CONV_CH, F32),
    ]
    return pl.pallas_call(
        _inproj_kernel,
        grid=(rows // ROW_TILE,),
        in_specs=[row(D_MODEL), _const_spec((1, D_MODEL)), _weight_spec(wq.shape), _weight_spec(wkv.shape),
                  _weight_spec(wqi.shape), _weight_spec(wkw.shape), _weight_spec(wu.shape),
                  _const_spec((1, HEAD_DIM)), _const_spec((1, HEAD_DIM))],
        out_specs=[row(w) for w, _ in outs],
        out_shape=[jax.ShapeDtypeStruct((rows, w), dt) for w, dt in outs],
        compiler_params=_params(("arbitrary",)),
        name="in_projection",
    )(x_all, g_attn.reshape(1, D_MODEL), wq, wkv, wqi, wkw, wu, g_q.reshape(1, HEAD_DIM), g_k.reshape(1, HEAD_DIM))


def _t5_bias_of(dist, rb_ref, h):
    n = jnp.maximum(dist, 0)
    max_exact = N_BUCKETS // 2
    nf = jnp.maximum(n, 1).astype(F32)
    large = max_exact + (jnp.log(nf / max_exact) / math.log(MAX_DISTANCE / max_exact)
                         * (N_BUCKETS - max_exact)).astype(I32)
    large = jnp.minimum(large, N_BUCKETS - 1)
    bucket = jnp.where(n < max_exact, n, large)
    out = jnp.zeros(dist.shape, F32)
    for k in range(N_BUCKETS):
        out = jnp.where(bucket == k, rb_ref[k, h], out)
    return out


def _bias_kernel(rb_ref, pb_ref, dn_ref, dc_ref):
    r2 = lax.broadcasted_iota(I32, (2 * QB, QB), 0)
    c2 = lax.broadcasted_iota(I32, (2 * QB, QB), 1)
    r1 = lax.broadcasted_iota(I32, (SUBLANES, LANES), 0)
    c1 = lax.broadcasted_iota(I32, (SUBLANES, LANES), 1)
    for h in range(N_HEADS):
        pb_ref[h] = _t5_bias_of(QB + c2 - r2, rb_ref, h)
        dn_ref[h] = _t5_bias_of(PAGE_SIZE + r1 - c1, rb_ref, h)
        dc_ref[h] = _t5_bias_of(r1 - c1, rb_ref, h)


def _bias_tiles(rel_bias):
    return pl.pallas_call(
        _bias_kernel,
        in_specs=[pl.BlockSpec(memory_space=pltpu.SMEM)],
        out_shape=[jax.ShapeDtypeStruct((N_HEADS, 2 * QB, QB), F32),
                   jax.ShapeDtypeStruct((N_HEADS, SUBLANES, LANES), F32),
                   jax.ShapeDtypeStruct((N_HEADS, SUBLANES, LANES), F32)],
        name="t5_bias_tiles",
    )(rel_bias)


def _prompt_attn_kernel(qi_ref, q_ref, kw_ref, ki_ref, k_ref, v_ref, pb_ref, cb_ref, o_ref,
                        key_sc, acc_sc, m_sc, l_sc, *, topk, lc):
    i = pl.program_id(1)
    q_lo = i * QB
    n_chunks = (q_lo + QB + lc - 1) // lc
    w_t = kw_ref[...].T
    row_l = lax.broadcasted_iota(I32, (lc, QB), 0)
    qpos_l = lax.broadcasted_iota(I32, (lc, QB), 1) + q_lo

    def score_chunk(c, carry):
        base = pl.multiple_of(c * lc, lc)
        kc = ki_ref[pl.ds(base, lc), :]
        acc = jnp.zeros((lc, QB), F32)
        for h in range(IDX_HEADS):
            s = _dot_nt(kc, qi_ref[:, h * IDX_DIM:(h + 1) * IDX_DIM])
            acc = acc + w_t[IDX_DIM + h:IDX_DIM + h + 1, :] * jnp.maximum(s, 0.0)
        key = jnp.where(row_l + base <= qpos_l, _order_key(acc), INT_MIN)
        key_sc[pl.ds(base, lc), :] = key
        return carry

    lax.fori_loop(0, n_chunks, score_chunk, 0)

    def count_keys(pred):
        def count_chunk(c, cnt):
            base = pl.multiple_of(c * lc, lc)
            hit = pred(key_sc[pl.ds(base, lc), :], row_l + base).astype(I32)
            return cnt + jnp.sum(hit.reshape(lc // SUBLANES, SUBLANES, QB), axis=0)

        cnt = lax.fori_loop(0, n_chunks, count_chunk, jnp.zeros((SUBLANES, QB), I32))
        return jnp.sum(cnt, axis=0, keepdims=True)

    def bisect(bit, thr):
        cand = thr + lax.shift_left(jnp.int32(1), 31 - bit)
        return jnp.where(count_keys(lambda k, l: k >= cand) >= topk, cand, thr)

    thr = lax.fori_loop(0, 32, bisect, jnp.full((1, QB), INT_MIN, I32))

    surplus = count_keys(lambda k, l: k >= thr) - topk
    tied = (surplus > 0) & (thr > INT_MIN)

    @pl.when(jnp.max(tied.astype(I32)) > 0)
    def _():
        need = topk - count_keys(lambda k, l: k > thr)
        pos_bits = key_sc.shape[0].bit_length()

        def bisect_pos(bit, cut):
            cand = cut + lax.shift_left(jnp.int32(1), pos_bits - 1 - bit)
            return jnp.where(count_keys(lambda k, l: (k == thr) & (l < cand)) < need, cand, cut)

        cut = lax.fori_loop(0, pos_bits, bisect_pos, jnp.zeros((1, QB), I32))

        def drop_chunk(c, carry):
            base = pl.multiple_of(c * lc, lc)
            k = key_sc[pl.ds(base, lc), :]
            drop = tied & (k == thr) & (row_l + base > cut)
            key_sc[pl.ds(base, lc), :] = jnp.where(drop, INT_MIN, k)
            return carry

        lax.fori_loop(0, n_chunks, drop_chunk, 0)

    thr = jnp.maximum(thr, INT_MIN + 1)

    m_sc[...] = jnp.full(m_sc.shape, M_INIT, F32)
    l_sc[...] = jnp.zeros(l_sc.shape, F32)
    acc_sc[...] = jnp.zeros(acc_sc.shape, F32)

    def attend(base, rows, sel, bias_of):
        kc = k_ref[pl.ds(base, rows), :]
        vc = v_ref[pl.ds(base, rows), :]
        for n in range(N_KV_HEADS):
            kn = kc[:, n * HEAD_DIM:(n + 1) * HEAD_DIM]
            vn = vc[:, n * HEAD_DIM:(n + 1) * HEAD_DIM]
            for g in range(GROUP):
                h = n * GROUP + g
                lg = _dot_nt(kn, q_ref[:, h * HEAD_DIM:(h + 1) * HEAD_DIM]) * ATTN_SCALE + bias_of(h)
                lg = jnp.where(sel, lg, NEG_BIG)
                m_old = m_sc[h:h + 1, :]
                m_new = jnp.maximum(m_old, jnp.max(lg, axis=0, keepdims=True))
                p = jnp.exp(lg - m_new)
                alpha = jnp.exp(m_old - m_new)
                l_sc[h:h + 1, :] = alpha * l_sc[h:h + 1, :] + jnp.sum(p, axis=0, keepdims=True)
                acc_sc[h] = alpha * acc_sc[h] + _dot_tn(vn, p.astype(BF16))
                m_sc[h:h + 1, :] = m_new

    far_hi = jnp.maximum(q_lo - QB, 0)
    n_far = (far_hi + lc - 1) // lc

    def far_chunk(c, carry):
        base = pl.multiple_of(c * lc, lc)
        sel = (key_sc[pl.ds(base, lc), :] >= thr) & (row_l + base < far_hi)
        attend(base, lc, sel, lambda h: cb_ref[h])
        return carry

    lax.fori_loop(0, n_far, far_chunk, 0)

    prev_lo = pl.multiple_of(jnp.maximum(q_lo - QB, 0), QB)
    sel_prev = (key_sc[pl.ds(prev_lo, QB), :] >= thr) & (i > 0)
    attend(prev_lo, QB, sel_prev, lambda h: pb_ref[h, :QB, :])
    diag_lo = pl.multiple_of(q_lo, QB)
    sel_diag = key_sc[pl.ds(diag_lo, QB), :] >= thr
    attend(diag_lo, QB, sel_diag, lambda h: pb_ref[h, QB:, :])

    for h in range(N_HEADS):
        o_t = acc_sc[h] / l_sc[h:h + 1, :]
        o_ref[:, h * HEAD_DIM:(h + 1) * HEAD_DIM] = o_t.T.astype(BF16)


def _prompt_attention(dm, q_b, qi_b, kw, ki_b, k_b, v_b, pb, cb):
    nq, tp = dm.nq, dm.tp
    topk = min(TOPK_MAX, dm.seq // 4)
    m = max(d for d in (1, 2, 3, 4) if nq % d == 0)
    lc = m * QB
    qblk = lambda w: pl.BlockSpec((QB, w), lambda b, i: (b * nq + i, 0))
    kblk = lambda w: pl.BlockSpec((tp, w), lambda b, i: (b, 0))
    return pl.pallas_call(
        functools.partial(_prompt_attn_kernel, topk=topk, lc=lc),
        grid=(dm.batch, nq),
        in_specs=[qblk(IDX_HEADS * IDX_DIM), qblk(ATTN_WIDTH), qblk(LANES),
                  kblk(IDX_DIM), kblk(KV_WIDTH), kblk(KV_WIDTH),
                  _const_spec(pb.shape), _const_spec(cb.shape)],
        out_specs=qblk(ATTN_WIDTH),
        out_shape=jax.ShapeDtypeStruct((dm.prompt_rows, ATTN_WIDTH), BF16),
        scratch_shapes=[pltpu.VMEM((tp, QB), I32), pltpu.VMEM((N_HEADS, HEAD_DIM, QB), F32),
                        pltpu.VMEM((N_HEADS, QB), F32), pltpu.VMEM((N_HEADS, QB), F32)],
        compiler_params=_params(("arbitrary", "arbitrary")),
        name="prompt_attention",
    )(qi_b, q_b, kw, ki_b, k_b, v_b, pb, cb)


DEC_PAGES_MAX = 16


def _decode_kernel(pt_ref, qi_ref, w_ref, q_ref, kicur_ref, kcur_ref, vcur_ref, bfar_ref, bnear_ref, bcur_ref, dup_ref,
                   *refs, topk, n_pages, dec_seq, pps):
    ki_refs = refs[:pps]
    k_refs = refs[pps:2 * pps]
    v_refs = refs[2 * pps:3 * pps]
    o_ref, key_sc, thr_sc, m_sc, l_sc, acc_sc = refs[3 * pps:]
    n_groups = n_pages // pps
    s = pl.program_id(1)
    n_rows = N_KV_HEADS * GROUP * SUBLANES
    n_cols = N_KV_HEADS * PAGE_SIZE

    def scores(keys_t_bf):
        sc = _dot(qi_ref[0], keys_t_bf)
        acc = jnp.zeros((SUBLANES, LANES), F32)
        for h in range(IDX_HEADS):
            rows = slice(h * SUBLANES, (h + 1) * SUBLANES)
            acc = acc + w_ref[0, rows, :] * jnp.maximum(sc[rows, :], 0.0)
        return _order_key(acc)

    @pl.when(s < n_groups)
    def _():
        for j in range(pps):
            key_sc[s * pps + j] = scores(ki_refs[j][...].astype(BF16))

    @pl.when(s == n_groups - 1)
    def _():
        r = lax.broadcasted_iota(I32, (SUBLANES, LANES), 0)
        c = lax.broadcasted_iota(I32, (SUBLANES, LANES), 1)
        key_sc[n_pages] = jnp.where((c <= r) & (c < dec_seq), scores(kicur_ref[0]), INT_MIN)

        page_of = lax.broadcasted_iota(I32, (pps, SUBLANES, LANES), 0)
        lane_of = lax.broadcasted_iota(I32, (pps, SUBLANES, LANES), 2)

        def count_keys(pred):
            cnt = pred(key_sc[n_pages], c + n_pages * PAGE_SIZE).astype(I32)
            for blk in range(n_groups):
                pos = (page_of + blk * pps) * PAGE_SIZE + lane_of
                cnt = cnt + jnp.sum(pred(key_sc[pl.ds(blk * pps, pps)], pos).astype(I32), axis=0)
            return jnp.sum(cnt, axis=1, keepdims=True)

        def bisect(bit, thr):
            cand = thr + lax.shift_left(jnp.int32(1), 31 - bit)
            return jnp.where(count_keys(lambda k, l: k >= cand) >= topk, cand, thr)

        thr = lax.fori_loop(0, 32, bisect, jnp.full((SUBLANES, 1), INT_MIN, I32))

        surplus = count_keys(lambda k, l: k >= thr) - topk
        tied = (surplus > 0) & (thr > INT_MIN)

        @pl.when(jnp.max(tied.astype(I32)) > 0)
        def _():
            need = topk - count_keys(lambda k, l: k > thr)
            pos_bits = ((n_pages + 1) * PAGE_SIZE).bit_length()

            def bisect_pos(bit, cut):
                cand = cut + lax.shift_left(jnp.int32(1), pos_bits - 1 - bit)
                return jnp.where(count_keys(lambda k, l: (k == thr) & (l < cand)) < need, cand, cut)

            cut = lax.fori_loop(0, pos_bits, bisect_pos, jnp.zeros((SUBLANES, 1), I32))

            def drop_page(j, carry):
                k = key_sc[j]
                drop = tied & (k == thr) & (c + j * PAGE_SIZE > cut)
                key_sc[j] = jnp.where(drop, INT_MIN, k)
                return carry

            lax.fori_loop(0, n_pages + 1, drop_page, 0)

        thr_sc[...] = jnp.broadcast_to(jnp.maximum(thr, INT_MIN + 1), (SUBLANES, LANES))
        m_sc[...] = jnp.full(m_sc.shape, M_INIT, F32)
        l_sc[...] = jnp.zeros(l_sc.shape, F32)
        acc_sc[...] = jnp.zeros(acc_sc.shape, F32)

    def attend(key_tiles, k_of, v_of, bias_of):
        row_head = lax.broadcasted_iota(I32, (n_rows, n_cols), 0) // (GROUP * SUBLANES)
        col_head = lax.broadcasted_iota(I32, (n_rows, n_cols), 1) % N_KV_HEADS
        own_head = row_head == col_head
        thr = thr_sc[...]
        lgs = []
        for j, kt in enumerate(key_tiles):
            picked = jnp.where(kt >= thr, 1.0, 0.0).astype(BF16)
            picked = _dot(picked, dup_ref[...])
            picked = jnp.concatenate([picked] * (n_rows // SUBLANES), axis=0)
            lg = _dot_nt(q_ref[0], k_of(j)) * ATTN_SCALE + bias_of(j)
            lgs.append(jnp.where(jnp.where(own_head, picked, 0.0) > 0.5, lg, NEG_BIG))
        m_old = m_sc[...]
        m_new = jnp.maximum(m_old, jnp.max(functools.reduce(jnp.maximum, lgs), axis=1, keepdims=True))
        prs = [jnp.exp(lg - m_new) for lg in lgs]
        alpha = jnp.exp(m_old - m_new)
        l_sc[...] = alpha * l_sc[...] + jnp.sum(functools.reduce(jnp.add, prs), axis=1, keepdims=True)
        pv = functools.reduce(jnp.add, [_dot(pr.astype(BF16), v_of(j)) for j, pr in enumerate(prs)])
        acc_sc[...] = alpha * acc_sc[...] + pv
        m_sc[...] = m_new

    @pl.when(s >= n_groups)
    def _():
        g = s - n_groups
        last_group = g == n_groups - 1

        def bias_of(j):
            if j == pps - 1:
                return jnp.where(last_group, bnear_ref[...], bfar_ref[...])
            return bfar_ref[...]

        attend([key_sc[g * pps + j] for j in range(pps)],
               lambda j: k_refs[j][...].astype(BF16), lambda j: v_refs[j][...].astype(BF16), bias_of)

    @pl.when(s == 2 * n_groups - 1)
    def _():
        attend([key_sc[n_pages]], lambda j: kcur_ref[0], lambda j: vcur_ref[0], lambda j: bcur_ref[...])
        o_ref[0] = acc_sc[...] / l_sc[...]


def _decode_attention(dm, page_table, cache_k, cache_v, cache_kidx, qi_s, w_col, ki_cur, q_s, k_cur, v_cur,
                      bfar, bnear, bcur):
    db, np_ = dm.dec_batch, dm.n_pages
    topk = min(TOPK_MAX, (dm.past_len + dm.dec_seq) // 4)
    rows_hq = IDX_HEADS * SUBLANES
    n_rows = N_KV_HEADS * GROUP * SUBLANES
    n_cols = N_KV_HEADS * PAGE_SIZE
    pps = max(d for d in range(1, DEC_PAGES_MAX + 1) if np_ % d == 0)
    ng = np_ // pps
    ki_t = jnp.swapaxes(cache_kidx, 2, 3)
    k_rows = cache_k.reshape(cache_k.shape[:2] + (n_cols, HEAD_DIM))
    v_rows = cache_v.reshape(cache_v.shape[:2] + (n_cols, HEAD_DIM))
    dup = (jnp.arange(n_cols)[None, :] // N_KV_HEADS == jnp.arange(PAGE_SIZE)[:, None]).astype(BF16)
    per_b = lambda *shape: pl.BlockSpec((1,) + shape, lambda b, s, pt: (b,) + (0,) * len(shape))
    const = lambda shape: pl.BlockSpec(shape, lambda b, s, pt: (0,) * len(shape))
    ki_page = lambda j: pl.BlockSpec(
        (None, None, IDX_DIM, PAGE_SIZE), lambda b, s, pt: (0, pt[b, jnp.minimum(s, ng - 1) * pps + j], 0, 0))
    kv_page = lambda j: pl.BlockSpec(
        (None, None, n_cols, HEAD_DIM), lambda b, s, pt: (0, pt[b, jnp.maximum(s - ng, 0) * pps + j], 0, 0))
    pages = range(pps)
    return pl.pallas_call(
        functools.partial(_decode_kernel, topk=topk, n_pages=np_, dec_seq=dm.dec_seq, pps=pps),
        grid_spec=pltpu.PrefetchScalarGridSpec(
            num_scalar_prefetch=1, grid=(db, 2 * ng),
            in_specs=[per_b(rows_hq, IDX_DIM), per_b(rows_hq, 1), per_b(n_rows, HEAD_DIM),
                      per_b(IDX_DIM, PAGE_SIZE), per_b(n_cols, HEAD_DIM), per_b(n_cols, HEAD_DIM),
                      const(bfar.shape), const(bnear.shape), const(bcur.shape), const(dup.shape)]
                     + [ki_page(j) for j in pages] + [kv_page(j) for j in pages] + [kv_page(j) for j in pages],
            out_specs=per_b(n_rows, HEAD_DIM),
            scratch_shapes=[pltpu.VMEM((np_ + 1, SUBLANES, LANES), I32), pltpu.VMEM((SUBLANES, LANES), I32),
                            pltpu.VMEM((n_rows, 1), F32), pltpu.VMEM((n_rows, 1), F32),
                            pltpu.VMEM((n_rows, HEAD_DIM), F32)]),
        out_shape=jax.ShapeDtypeStruct((db, n_rows, HEAD_DIM), F32),
        compiler_params=_params(("arbitrary", "arbitrary")),
        name="decode_attention",
    )(page_table, qi_s, w_col, q_s, ki_cur, k_cur, v_cur, bfar, bnear, bcur, dup,
      *([ki_t] * pps), *([k_rows] * pps), *([v_rows] * pps))


CONV_CT = 256


def _conv_ln_swish(ext_ref, rows, w_ref, b_ref, g_ref, beta_ref, y_sc, out_ref_setter):
    off = CONV_HALO - (CONV_W - 1)
    for ct in range(CONV_CH // CONV_CT):
        cs = slice(ct * CONV_CT, (ct + 1) * CONV_CT)
        acc = jnp.zeros((rows, CONV_CT), F32)
        for j in range(CONV_W):
            acc = acc + ext_ref[pl.ds(off + j, rows), cs] * w_ref[j:j + 1, cs]
        y_sc[:, cs] = acc + b_ref[:, cs]
    y = y_sc[...]
    mu = jnp.mean(y, axis=-1, keepdims=True)
    d = y - mu
    var = jnp.mean(d * d, axis=-1, keepdims=True)
    yn = d * lax.rsqrt(var + EPS) * g_ref[...] + beta_ref[...]
    out_ref_setter(yn * jax.nn.sigmoid(yn))


def _prompt_conv_kernel(prev_ref, cur_ref, w_ref, b_ref, g_ref, beta_ref, c_ref, ext_sc, y_sc):
    i = pl.program_id(1)
    halo = prev_ref[QB - CONV_HALO:, :]
    ext_sc[:CONV_HALO, :] = jnp.where(i > 0, halo, 0.0)
    ext_sc[CONV_HALO:, :] = cur_ref[...]

    def put(c):
        c_ref[...] = c.astype(BF16)

    _conv_ln_swish(ext_sc, QB, w_ref, b_ref, g_ref, beta_ref, y_sc, put)


def _sample_conv_kernel(ext_ref, w_ref, b_ref, g_ref, beta_ref, c_ref, y_sc, *, rows):
    def put(c):
        c_ref[0] = c

    _conv_ln_swish(ext_ref.at[0], rows, w_ref, b_ref, g_ref, beta_ref, y_sc, put)


def _conv_weights(dw_w, dw_b, ln_g, ln_b):
    return (dw_w, dw_b.reshape(1, CONV_CH), ln_g.reshape(1, CONV_CH), ln_b.reshape(1, CONV_CH))


def _prompt_conv(dm, a_all, cw):
    nq = dm.nq
    blk = lambda f: pl.BlockSpec((QB, CONV_CH), f)
    return pl.pallas_call(
        _prompt_conv_kernel,
        grid=(dm.batch, nq),
        in_specs=[blk(lambda b, i: (jnp.maximum(b * nq + i - 1, 0), 0)), blk(lambda b, i: (b * nq + i, 0)),
                  _const_spec((CONV_W, CONV_CH))] + [_const_spec((1, CONV_CH))] * 3,
        out_specs=blk(lambda b, i: (b * nq + i, 0)),
        out_shape=jax.ShapeDtypeStruct((dm.prompt_rows, CONV_CH), BF16),
        scratch_shapes=[pltpu.VMEM((CONV_HALO + QB, CONV_CH), F32), pltpu.VMEM((QB, CONV_CH), F32)],
        compiler_params=_params(("arbitrary", "arbitrary")),
        name="prompt_conv",
    )(a_all, a_all, *cw)


def _sample_conv(dm, ext_s, cw):
    rows = dm.dec_seq
    tot = CONV_HALO + rows
    return pl.pallas_call(
        functools.partial(_sample_conv_kernel, rows=rows),
        grid=(dm.dec_batch,),
        in_specs=[pl.BlockSpec((1, tot, CONV_CH), lambda b: (b, 0, 0)), _const_spec((CONV_W, CONV_CH))]
                 + [_const_spec((1, CONV_CH))] * 3,
        out_specs=pl.BlockSpec((1, rows, CONV_CH), lambda b: (b, 0, 0)),
        out_shape=jax.ShapeDtypeStruct((dm.dec_batch, rows, CONV_CH), F32),
        scratch_shapes=[pltpu.VMEM((rows, CONV_CH), F32)],
        compiler_params=_params(("arbitrary",)),
        name="sample_conv",
    )(ext_s, *cw)


def _outproj_kernel(o_ref, c_ref, x_ref, wo_ref, wc_ref, g_ref, wpq_ref, h_ref, hn_ref, qp_ref):
    h = x_ref[...] + _dot(o_ref[...], wo_ref[...]) + _dot(c_ref[...], wc_ref[...])
    h_ref[...] = h
    ms = jnp.mean(h * h, axis=-1, keepdims=True)
    hn = h * lax.rsqrt(ms + EPS) * g_ref[...]
    hn_ref[...] = hn
    qp_ref[...] = _dot(hn.astype(BF16), wpq_ref[...]).astype(BF16)


def _out_projection(o_all, c_all, x_all, w_out, g_ffn, peer_wq):
    rows = x_all.shape[0]
    wo = w_out[:ATTN_WIDTH].astype(BF16)
    wc = w_out[ATTN_WIDTH:].astype(BF16)
    wpq = peer_wq.astype(BF16)
    row = lambda w: pl.BlockSpec((ROW_TILE, w), lambda i: (i, 0))
    pq = PEER_HEADS * PEER_KDIM
    return pl.pallas_call(
        _outproj_kernel,
        grid=(rows // ROW_TILE,),
        in_specs=[row(ATTN_WIDTH), row(CONV_CH), row(D_MODEL), _weight_spec(wo.shape), _weight_spec(wc.shape),
                  _const_spec((1, D_MODEL)), _weight_spec(wpq.shape)],
        out_specs=[row(D_MODEL), row(D_MODEL), row(pq)],
        out_shape=[jax.ShapeDtypeStruct((rows, D_MODEL), F32), jax.ShapeDtypeStruct((rows, D_MODEL), F32),
                   jax.ShapeDtypeStruct((rows, pq), BF16)],
        compiler_params=_params(("arbitrary",)),
        name="out_projection",
    )(o_all, c_all, x_all, wo, wc, g_ffn.reshape(1, D_MODEL), wpq)


def _extract_top(vals, pos, payload, count):
    big = jnp.int32(2 ** 30)
    out_v, out_p = [], []
    for _ in range(count):
        m = jnp.max(vals, axis=0, keepdims=True)
        first = jnp.min(jnp.where(vals == m, pos, big), axis=0, keepdims=True)
        hit = pos == first
        out_v.append(m)
        out_p.append(first if payload is None else jnp.sum(jnp.where(hit, payload, 0), axis=0, keepdims=True))
        vals = jnp.where(hit, -jnp.inf, vals)
    return jnp.concatenate(out_v, axis=0), jnp.concatenate(out_p, axis=0)


def _product_candidates(v1, i1, v2, i2):
    k = PEER_TOPK
    r8 = lax.broadcasted_iota(I32, (SUBLANES, LANES), 0)
    r16 = lax.broadcasted_iota(I32, (k, LANES), 0)
    vals = [v1[0:1] + v2]
    poss = [r16]
    idxs = [i1[0:1] * PEER_NKEYS + i2]
    for a in (1, 2, 3):
        vals.append(v1[a:a + 1] + v2[:SUBLANES])
        poss.append(a * k + r8)
        idxs.append(i1[a:a + 1] * PEER_NKEYS + i2[:SUBLANES])
    for b in (0, 1, 2):
        vals.append(jnp.where(r8 >= 4, v1[:SUBLANES] + v2[b:b + 1], -jnp.inf))
        poss.append(jnp.where(r8 >= 4, r8 * k + b, -1))
        idxs.append(i1[:SUBLANES] * PEER_NKEYS + i2[b:b + 1])
    vals.append(v1[SUBLANES:] + v2[0:1])
    poss.append((r8 + SUBLANES) * k)
    idxs.append(i1[SUBLANES:] * PEER_NKEYS + i2[0:1])
    cat = lambda xs: jnp.concatenate(xs, axis=0)
    return cat(vals), cat(poss), cat(idxs)


def _route_kernel(qp_ref, sk_ref, e_ref, g_ref):
    half = PEER_KDIM // 2
    key_id = lax.broadcasted_iota(I32, (PEER_NKEYS, LANES), 0)
    for h in range(PEER_HEADS):
        tops = []
        for c in range(2):
            qs = qp_ref[:, h * PEER_KDIM + c * half:h * PEER_KDIM + (c + 1) * half]
            s = _dot_nt(sk_ref[c], qs)
            tops.append(_extract_top(s, key_id, None, PEER_TOPK))
        (v1, i1), (v2, i2) = tops
        cand, cpos, cidx = _product_candidates(v1, i1, v2, i2)
        vals, experts = _extract_top(cand, cpos, cidx, PEER_TOPK)
        ex = jnp.exp(vals - vals[0:1, :])
        g_ref[h] = ex / jnp.sum(ex, axis=0, keepdims=True)
        e_ref[h] = experts


def _peer_route(qp, sub_keys):
    rows = qp.shape[0]
    sk = sub_keys.astype(BF16)
    out = pl.BlockSpec((PEER_HEADS, PEER_TOPK, LANES), lambda i: (0, 0, i))
    return pl.pallas_call(
        _route_kernel,
        grid=(rows // LANES,),
        in_specs=[pl.BlockSpec((LANES, PEER_HEADS * PEER_KDIM), lambda i: (i, 0)), _const_spec(sk.shape)],
        out_specs=[out, out],
        out_shape=[jax.ShapeDtypeStruct((PEER_HEADS, PEER_TOPK, rows), I32),
                   jax.ShapeDtypeStruct((PEER_HEADS, PEER_TOPK, rows), F32)],
        compiler_params=_params(("arbitrary",)),
        name="peer_route",
    )(qp, sk)


def _pack_bf16_pairs(t):
    bits = lax.bitcast_convert_type(t.astype(BF16), jnp.uint16).astype(jnp.uint32)
    half = t.shape[1] // 2
    return lax.bitcast_convert_type((bits[:, :half] << 16) | bits[:, half:], I32)


def _hi_half(w):
    return pltpu.bitcast(w & jnp.int32(-65536), F32)


def _lo_half(w):
    return pltpu.bitcast(lax.shift_left(w, jnp.int32(16)), F32)


def _peer_ffn_kernel(idx_ref, gate_ref, hn_ref, h_ref, uv_ref, y_ref, buf, out_sc, sem):
    n_groups = PEER_TOK_BLOCK // PEER_UNROLL
    half = D_MODEL // 2

    n_tiles = half // LANES
    per_tile = PEER_SLOTS // (2 * n_tiles)

    def issue(tok, slot, ks):
        for k in ks:
            pltpu.make_async_copy(uv_ref.at[idx_ref[tok, k]], buf.at[slot, pl.ds(k, 1)], sem.at[slot]).start(
                priority=k % PEER_DMA_QUEUES)

    def wait(slot):
        pltpu.make_async_copy(uv_ref.at[pl.ds(0, PEER_SLOTS), 0], buf.at[slot], sem.at[slot]).wait()

    for j in range(PEER_AHEAD):
        issue(j, j % PEER_BUFS, range(PEER_SLOTS))

    def group(gi, last):
        t0 = pl.multiple_of(gi * PEER_UNROLL, PEER_UNROLL)
        gates_t = gate_ref[pl.ds(t0, PEER_UNROLL), :].T
        x_grp = hn_ref[pl.ds(t0, PEER_UNROLL), :]
        for j in range(PEER_UNROLL):
            slot = j % PEER_BUFS
            ahead = j + PEER_AHEAD
            prefetch = ahead < PEER_UNROLL or not last

            def issue_part(part):
                if prefetch:
                    issue(t0 + ahead, ahead % PEER_BUFS, range(part * per_tile, (part + 1) * per_tile))

            wait(slot)
            hacc = jnp.zeros((PEER_SLOTS, LANES), F32)
            for c in range(n_tiles):
                issue_part(c)
                lo_cols = slice(c * LANES, (c + 1) * LANES)
                hi_cols = slice(half + c * LANES, half + (c + 1) * LANES)
                w = buf[slot, :, lo_cols]
                hacc = hacc + _hi_half(w) * x_grp[j:j + 1, lo_cols] + _lo_half(w) * x_grp[j:j + 1, hi_cols]
            hcol = jnp.sum(hacc, axis=1, keepdims=True)
            act = gates_t[:, j:j + 1] * jax.nn.gelu(hcol)
            act_b = jnp.broadcast_to(act, (PEER_SLOTS, LANES))
            for c in range(n_tiles):
                issue_part(n_tiles + c)
                lo_cols = slice(c * LANES, (c + 1) * LANES)
                hi_cols = slice(half + c * LANES, half + (c + 1) * LANES)
                w = buf[slot, :, half + c * LANES:half + (c + 1) * LANES]
                out_sc[j:j + 1, lo_cols] = jnp.sum(_hi_half(w) * act_b, axis=0, keepdims=True)
                out_sc[j:j + 1, hi_cols] = jnp.sum(_lo_half(w) * act_b, axis=0, keepdims=True)
        y_ref[pl.ds(t0, PEER_UNROLL), :] = h_ref[pl.ds(t0, PEER_UNROLL), :] + out_sc[...]

    def full_group(gi, carry):
        group(gi, last=False)
        return carry

    lax.fori_loop(0, n_groups - 1, full_group, 0)
    group(n_groups - 1, last=True)


def _peer_ffn(experts_tok, gates_tok, hn2, h, uv):
    rows = h.shape[0]
    tb = PEER_TOK_BLOCK
    row = lambda w: pl.BlockSpec((tb, w), lambda i: (i, 0))
    return pl.pallas_call(
        _peer_ffn_kernel,
        grid=(rows // tb,),
        in_specs=[pl.BlockSpec((tb, PEER_SLOTS), lambda i: (i, 0), memory_space=pltpu.SMEM),
                  row(PEER_SLOTS), row(D_MODEL), row(D_MODEL), pl.BlockSpec(memory_space=pl.ANY)],
        out_specs=row(D_MODEL),
        out_shape=jax.ShapeDtypeStruct((rows, D_MODEL), F32),
        scratch_shapes=[pltpu.VMEM((PEER_BUFS, PEER_SLOTS, D_MODEL), I32), pltpu.VMEM((PEER_UNROLL, D_MODEL), F32),
                        pltpu.SemaphoreType.DMA((PEER_BUFS,))],
        compiler_params=pltpu.CompilerParams(dimension_semantics=("arbitrary",), vmem_limit_bytes=VMEM_LIMIT,
                                             disable_bounds_checks=True),
        name="peer_ffn",
    )(experts_tok, gates_tok, hn2, h, uv)


def _step(dm, x_prompt, x_sample, cache_k, cache_v, cache_kidx, state_conv, page_table, meta_tokens, rel_bias,
          g_attn, w_in, g_q, g_k, conv_dw_w, conv_dw_b, conv_ln_g, conv_ln_b, w_out, g_ffn, peer_wq, peer_subkeys,
          peer_u, peer_v):
    bsz, t, tp, db, ds = dm.batch, dm.t_prompt, dm.tp, dm.dec_batch, dm.dec_seq
    pr, sr = dm.prompt_rows, dm.sample_rows

    meta = jnp.broadcast_to(meta_tokens[None].astype(x_prompt.dtype), (bsz, N_META, D_MODEL))
    xp = jnp.concatenate([meta, x_prompt, jnp.zeros((bsz, tp - t, D_MODEL), x_prompt.dtype)], axis=1)
    x_all = jnp.concatenate([xp.reshape(pr, D_MODEL), x_sample.reshape(sr, D_MODEL),
                             jnp.zeros((dm.rows - pr - sr, D_MODEL), x_prompt.dtype)], axis=0)

    q_b, k_f, v_f, k_b, v_b, qi_b, kw, ki_b, a_all = _in_projection(x_all, g_attn[0], w_in[0], g_q[0], g_k[0])

    pb, dn, dc = _bias_tiles(rel_bias)
    far = rel_bias[N_BUCKETS - 1]
    cb = jnp.broadcast_to(far[:, None, None], (N_HEADS, 1, QB))
    o_p = _prompt_attention(dm, q_b, qi_b, kw, ki_b, k_b, v_b, pb, cb)

    s0, s1 = pr, pr + sr
    qi_s = qi_b[s0:s1].reshape(db, ds, IDX_HEADS, IDX_DIM).transpose(0, 2, 1, 3).reshape(db, IDX_HEADS * ds, IDX_DIM)
    w_col = kw[s0:s1, IDX_DIM:IDX_DIM + IDX_HEADS].reshape(db, ds, IDX_HEADS).transpose(0, 2, 1)
    w_col = w_col.reshape(db, IDX_HEADS * ds, 1)
    q_s = q_b[s0:s1].reshape(db, ds, N_KV_HEADS, GROUP, HEAD_DIM).transpose(0, 2, 3, 1, 4)
    q_s = q_s.reshape(db, N_HEADS * ds, HEAD_DIM)
    pad_keys = lambda z: jnp.pad(z.reshape(db, ds, -1), ((0, 0), (0, PAGE_SIZE - ds), (0, 0)))
    kv_rows = lambda z: pad_keys(z).reshape(db, PAGE_SIZE * N_KV_HEADS, HEAD_DIM)
    by_rows = lambda z: jnp.repeat(z.reshape(N_HEADS * SUBLANES, LANES), N_KV_HEADS, axis=1)
    bfar = jnp.broadcast_to(far[:, None, None], (N_HEADS, SUBLANES, LANES))
    o_s = _decode_attention(
        dm, page_table, cache_k, cache_v, cache_kidx, qi_s, w_col, jnp.swapaxes(pad_keys(ki_b[s0:s1]), 1, 2), q_s,
        kv_rows(k_b[s0:s1]), kv_rows(v_b[s0:s1]), by_rows(bfar), by_rows(dn), by_rows(dc))
    o_s = o_s.reshape(db, N_HEADS, ds, HEAD_DIM).transpose(0, 2, 1, 3).reshape(sr, ATTN_WIDTH)

    cw = _conv_weights(conv_dw_w[0], conv_dw_b[0], conv_ln_g[0], conv_ln_b[0])
    c_p = _prompt_conv(dm, a_all, cw)
    a_s = a_all[s0:s1].reshape(db, ds, CONV_CH)
    ext_s = jnp.concatenate([jnp.zeros((db, CONV_HALO - (CONV_W - 1), CONV_CH), F32), state_conv[0], a_s], axis=1)
    c_s = _sample_conv(dm, ext_s, cw).reshape(sr, CONV_CH)

    tail = jnp.zeros((dm.rows - pr - sr, ATTN_WIDTH), BF16)
    o_all = jnp.concatenate([o_p, o_s.astype(BF16), tail], axis=0)
    c_all = jnp.concatenate([c_p, c_s.astype(BF16), tail], axis=0)
    h, hn2, qp = _out_projection(o_all, c_all, x_all, w_out[0], g_ffn[0], peer_wq[0])

    experts, gates = _peer_route(qp, peer_subkeys[0])
    experts_tok = experts.reshape(PEER_SLOTS, dm.rows).T
    gates_tok = gates.reshape(PEER_SLOTS, dm.rows).T
    uv = jnp.concatenate([_pack_bf16_pairs(peer_u[0]), _pack_bf16_pairs(peer_v[0])], axis=1)[:, None, :]
    y = _peer_ffn(experts_tok, gates_tok, hn2, h, uv)

    per_seq = lambda z, w: z[:pr].reshape(bsz, tp, w)[:, :t]
    y_prompt = per_seq(y, D_MODEL)[:, N_META:]
    y_sample = y[s0:s1].reshape(db, ds, D_MODEL)
    kv_shape = (N_KV_HEADS, HEAD_DIM)
    new_k_p = per_seq(k_f, KV_WIDTH).reshape(1, bsz, t, *kv_shape)
    new_v_p = per_seq(v_f, KV_WIDTH).reshape(1, bsz, t, *kv_shape)
    new_ki_p = per_seq(kw, LANES)[:, :, :IDX_DIM][None]
    new_conv_p = per_seq(a_all, CONV_CH)[:, t - (CONV_W - 1):][None]
    new_k_s = k_f[s0:s1].reshape(1, db, ds, *kv_shape)
    new_v_s = v_f[s0:s1].reshape(1, db, ds, *kv_shape)
    new_ki_s = kw[s0:s1, :IDX_DIM].reshape(1, db, ds, IDX_DIM)
    new_conv_s = ext_s[:, -(CONV_W - 1):][None]
    return (y_prompt, y_sample, new_k_p, new_v_p, new_ki_p, new_conv_p, new_k_s, new_v_s, new_ki_s, new_conv_s)


def kernel(x_prompt, x_sample, cache_k, cache_v, cache_kidx, state_conv, page_table, meta_tokens, rel_bias, g_attn,
           w_in, g_q, g_k, conv_dw_w, conv_dw_b, conv_ln_g, conv_ln_b, w_out, g_ffn, peer_wq, peer_subkeys, peer_u,
           peer_v):
    assert g_attn.shape[0] == 1, "single trunk layer"
    dm = Dims(batch=x_prompt.shape[0], seq=x_prompt.shape[1], dec_batch=x_sample.shape[0],
              dec_seq=x_sample.shape[1], past_len=page_table.shape[1] * PAGE_SIZE)
    assert dm.dec_seq == SUBLANES and dm.t_prompt >= CONV_W - 1
    return _step(dm, x_prompt, x_sample, cache_k, cache_v, cache_kidx, state_conv, page_table, meta_tokens, rel_bias,
                 g_attn, w_in, g_q, g_k, conv_dw_w, conv_dw_b, conv_ln_g, conv_ln_b, w_out, g_ffn, peer_wq,
                 peer_subkeys, peer_u, peer_v)
```

```python
import functools
import math
from typing import NamedTuple

import jax
import jax.numpy as jnp
from jax import lax
from jax.experimental import pallas as pl
from jax.experimental.pallas import tpu as pltpu

F32 = jnp.float32
BF16 = jnp.bfloat16
I32 = jnp.int32

D_MODEL = 2048
N_META = 16
N_HEADS = 8
HEAD_DIM = 128
N_KV_HEADS = 2
GROUP = N_HEADS // N_KV_HEADS
ATTN_WIDTH = N_HEADS * HEAD_DIM
KV_WIDTH = N_KV_HEADS * HEAD_DIM
CONV_CH = D_MODEL - ATTN_WIDTH
CONV_W = 31
IDX_HEADS = 16
IDX_DIM = 64
IDX_SCALE = (IDX_HEADS * IDX_DIM) ** -0.5
TOPK_MAX = 256
N_BUCKETS = 32
MAX_DISTANCE = 128
ATTN_SCALE = HEAD_DIM ** -0.5
PAGE_SIZE = 128
PEER_HEADS = 8
PEER_NKEYS = 128
PEER_KDIM = 128
PEER_TOPK = 16
PEER_SLOTS = PEER_HEADS * PEER_TOPK
EPS = 1e-6

LANES = 128
SUBLANES = 8
QB = 128
ROW_TILE = 256
CONV_HALO = 32
PEER_TOK_BLOCK = 256
PEER_UNROLL = 8
PEER_BUFS = 4
PEER_AHEAD = PEER_BUFS - 1
PEER_DMA_QUEUES = 2
INT_MIN = -(2 ** 31)
NEG_BIG = -1e30
M_INIT = -1e29
VMEM_LIMIT = 56 * 1024 * 1024

NT_DIMS = (((1,), (1,)), ((), ()))
TN_DIMS = (((0,), (0,)), ((), ()))


class Dims(NamedTuple):
    batch: int
    seq: int
    dec_batch: int
    dec_seq: int
    past_len: int

    @property
    def t_prompt(self):
        return self.seq + N_META

    @property
    def nq(self):
        return -(-self.t_prompt // QB)

    @property
    def tp(self):
        return self.nq * QB

    @property
    def prompt_rows(self):
        return self.batch * self.tp

    @property
    def sample_rows(self):
        return self.dec_batch * self.dec_seq

    @property
    def rows(self):
        r = self.prompt_rows + self.sample_rows
        return -(-r // ROW_TILE) * ROW_TILE

    @property
    def n_pages(self):
        return self.past_len // PAGE_SIZE


def _dot(a, b):
    return jnp.dot(a, b, preferred_element_type=F32)


def _dot_nt(a, b):
    return lax.dot_general(a, b, NT_DIMS, preferred_element_type=F32)


def _dot_tn(a, b):
    return lax.dot_general(a, b, TN_DIMS, preferred_element_type=F32)


def _params(sem):
    return pltpu.CompilerParams(dimension_semantics=sem, vmem_limit_bytes=VMEM_LIMIT)


def _const_spec(shape):
    nd = len(shape)
    return pl.BlockSpec(shape, lambda *_: (0,) * nd)


def _weight_spec(shape):
    nd = len(shape)
    return pl.BlockSpec(shape, lambda *_: (0,) * nd, pipeline_mode=pl.Buffered(1))


def _order_key(x):
    bits = pltpu.bitcast(x + 0.0, I32)
    return bits ^ ((bits >> 31) & 0x7FFFFFFF)


def _inproj_kernel(x_ref, ga_ref, wq_ref, wkv_ref, wqi_ref, wkw_ref, wu_ref, gq_ref, gk_ref,
                   q_ref, kf_ref, vf_ref, kb_ref, vb_ref, qi_ref, kw_ref, kib_ref, a_ref):
    x = x_ref[...]
    ms = jnp.mean(x * x, axis=-1, keepdims=True)
    hn = (x * lax.rsqrt(ms + EPS) * ga_ref[...]).astype(BF16)

    def head_norm(z, g):
        zm = jnp.mean(z * z, axis=-1, keepdims=True)
        return z * lax.rsqrt(zm + EPS) * g

    q = _dot(hn, wq_ref[...])
    for h in range(N_HEADS):
        sl = slice(h * HEAD_DIM, (h + 1) * HEAD_DIM)
        q_ref[:, sl] = (head_norm(q[:, sl], gq_ref[...]) * ATTN_SCALE).astype(BF16)
    kv = _dot(hn, wkv_ref[...])
    for n in range(N_KV_HEADS):
        sl = slice(n * HEAD_DIM, (n + 1) * HEAD_DIM)
        kn = head_norm(kv[:, sl], gk_ref[...])
        kf_ref[:, sl] = kn
        kb_ref[:, sl] = kn.astype(BF16)
    v = kv[:, KV_WIDTH:]
    vf_ref[...] = v
    vb_ref[...] = v.astype(BF16)
    qi = _dot(hn, wqi_ref[...]).astype(BF16)
    for h in range(IDX_HEADS):
        qi_ref[h] = qi[:, h * IDX_DIM:(h + 1) * IDX_DIM]
    kw = _dot(hn, wkw_ref[...])
    lane = lax.broadcasted_iota(I32, kw.shape, 1)
    kw = jnp.where(lane >= IDX_DIM, kw * IDX_SCALE, kw)
    kw_ref[...] = kw
    kib_ref[...] = kw[:, :IDX_DIM].astype(BF16)
    u = _dot(hn, wu_ref[...])
    a_ref[...] = u[:, :CONV_CH] * jax.nn.sigmoid(u[:, CONV_CH:])


def _in_projection(x_all, g_attn, w_in, g_q, g_k):
    rows = x_all.shape[0]
    o_q = ATTN_WIDTH
    o_k = o_q + KV_WIDTH
    o_v = o_k + KV_WIDTH
    o_qi = o_v + IDX_HEADS * IDX_DIM
    o_ki = o_qi + IDX_DIM
    o_wi = o_ki + IDX_HEADS
    wq = w_in[:, :o_q].astype(BF16)
    wkv = w_in[:, o_q:o_v].astype(BF16)
    wqi = w_in[:, o_v:o_qi].astype(BF16)
    wkw = jnp.pad(w_in[:, o_qi:o_wi], ((0, 0), (0, LANES - IDX_DIM - IDX_HEADS))).astype(BF16)
    wu = w_in[:, o_wi:].astype(BF16)
    row = lambda w: pl.BlockSpec((ROW_TILE, w), lambda i: (i, 0))
    outs = [
        (ATTN_WIDTH, BF16), (KV_WIDTH, F32), (KV_WIDTH, F32), (KV_WIDTH, BF16), (KV_WIDTH, BF16),
        None, (LANES, F32), (IDX_DIM, BF16), (CONV_CH, F32),
    ]
    qi_spec = pl.BlockSpec((IDX_HEADS, ROW_TILE, IDX_DIM), lambda i: (0, i, 0))
    qi_shape = jax.ShapeDtypeStruct((IDX_HEADS, rows, IDX_DIM), BF16)
    return pl.pallas_call(
        _inproj_kernel,
        grid=(rows // ROW_TILE,),
        in_specs=[row(D_MODEL), _const_spec((1, D_MODEL)), _weight_spec(wq.shape), _weight_spec(wkv.shape),
                  _weight_spec(wqi.shape), _weight_spec(wkw.shape), _weight_spec(wu.shape),
                  _const_spec((1, HEAD_DIM)), _const_spec((1, HEAD_DIM))],
        out_specs=[qi_spec if o is None else row(o[0]) for o in outs],
        out_shape=[qi_shape if o is None else jax.ShapeDtypeStruct((rows, o[0]), o[1]) for o in outs],
        compiler_params=_params(("arbitrary",)),
        name="in_projection",
    )(x_all, g_attn.reshape(1, D_MODEL), wq, wkv, wqi, wkw, wu, g_q.reshape(1, HEAD_DIM), g_k.reshape(1, HEAD_DIM))


def _t5_bias_of(dist, rb_ref, h):
    n = jnp.maximum(dist, 0)
    max_exact = N_BUCKETS // 2
    nf = jnp.maximum(n, 1).astype(F32)
    large = max_exact + (jnp.log(nf / max_exact) / math.log(MAX_DISTANCE / max_exact)
                         * (N_BUCKETS - max_exact)).astype(I32)
    large = jnp.minimum(large, N_BUCKETS - 1)
    bucket = jnp.where(n < max_exact, n, large)
    out = jnp.zeros(dist.shape, F32)
    for k in range(N_BUCKETS):
        out = jnp.where(bucket == k, rb_ref[k, h], out)
    return out


def _bias_kernel(rb_ref, pb_ref, dn_ref, dc_ref):
    r2 = lax.broadcasted_iota(I32, (2 * QB, QB), 0)
    c2 = lax.broadcasted_iota(I32, (2 * QB, QB), 1)
    r1 = lax.broadcasted_iota(I32, (SUBLANES, LANES), 0)
    c1 = lax.broadcasted_iota(I32, (SUBLANES, LANES), 1)
    for h in range(N_HEADS):
        pb_ref[h] = _t5_bias_of(QB + c2 - r2, rb_ref, h)
        dn_ref[h] = _t5_bias_of(PAGE_SIZE + r1 - c1, rb_ref, h)
        dc_ref[h] = _t5_bias_of(r1 - c1, rb_ref, h)


def _bias_tiles(rel_bias):
    return pl.pallas_call(
        _bias_kernel,
        in_specs=[pl.BlockSpec(memory_space=pltpu.SMEM)],
        out_shape=[jax.ShapeDtypeStruct((N_HEADS, 2 * QB, QB), F32),
                   jax.ShapeDtypeStruct((N_HEADS, SUBLANES, LANES), F32),
                   jax.ShapeDtypeStruct((N_HEADS, SUBLANES, LANES), F32)],
        name="t5_bias_tiles",
    )(rel_bias)


def _prompt_attn_kernel(qi_ref, q_ref, kw_ref, ki_ref, k_ref, v_ref, pb_ref, cb_ref, o_ref,
                        key_sc, acc_sc, m_sc, l_sc, *, topk, lc):
    i = pl.program_id(1)
    q_lo = i * QB
    n_chunks = (q_lo + QB + lc - 1) // lc
    w_t = kw_ref[...].T
    row_l = lax.broadcasted_iota(I32, (lc, QB), 0)
    qpos_l = lax.broadcasted_iota(I32, (lc, QB), 1) + q_lo

    def score_chunk(c, carry):
        base = pl.multiple_of(c * lc, lc)
        kc = ki_ref[pl.ds(base, lc), :]
        acc = jnp.zeros((lc, QB), F32)
        for hp in range(IDX_HEADS // 2):
            s2 = _dot_nt(kc, qi_ref[2 * hp:2 * hp + 2].reshape(2 * QB, IDX_DIM))
            for h in (2 * hp, 2 * hp + 1):
                s = s2[:, (h % 2) * QB:(h % 2 + 1) * QB]
                acc = acc + w_t[IDX_DIM + h:IDX_DIM + h + 1, :] * jnp.maximum(s, 0.0)
        key = jnp.where(row_l + base <= qpos_l, _order_key(acc), INT_MIN)
        key_sc[pl.ds(base, lc), :] = key
        return carry

    lax.fori_loop(0, n_chunks, score_chunk, 0)

    def count_keys(pred):
        def count_chunk(c, cnt):
            base = pl.multiple_of(c * lc, lc)
            hit = pred(key_sc[pl.ds(base, lc), :], row_l + base).astype(I32)
            return cnt + jnp.sum(hit.reshape(lc // SUBLANES, SUBLANES, QB), axis=0)

        cnt = lax.fori_loop(0, n_chunks, count_chunk, jnp.zeros((SUBLANES, QB), I32))
        return jnp.sum(cnt, axis=0, keepdims=True)

    def bisect(bit, thr):
        cand = thr + lax.shift_left(jnp.int32(1), 31 - bit)
        return jnp.where(count_keys(lambda k, l: k >= cand) >= topk, cand, thr)

    thr = lax.fori_loop(0, 32, bisect, jnp.full((1, QB), INT_MIN, I32))

    surplus = count_keys(lambda k, l: k >= thr) - topk
    tied = (surplus > 0) & (thr > INT_MIN)

    @pl.when(jnp.max(tied.astype(I32)) > 0)
    def _():
        need = topk - count_keys(lambda k, l: k > thr)
        pos_bits = key_sc.shape[0].bit_length()

        def bisect_pos(bit, cut):
            cand = cut + lax.shift_left(jnp.int32(1), pos_bits - 1 - bit)
            return jnp.where(count_keys(lambda k, l: (k == thr) & (l < cand)) < need, cand, cut)

        cut = lax.fori_loop(0, pos_bits, bisect_pos, jnp.zeros((1, QB), I32))

        def drop_chunk(c, carry):
            base = pl.multiple_of(c * lc, lc)
            k = key_sc[pl.ds(base, lc), :]
            drop = tied & (k == thr) & (row_l + base > cut)
            key_sc[pl.ds(base, lc), :] = jnp.where(drop, INT_MIN, k)
            return carry

        lax.fori_loop(0, n_chunks, drop_chunk, 0)

    thr = jnp.maximum(thr, INT_MIN + 1)

    m_sc[...] = jnp.full(m_sc.shape, M_INIT, F32)
    l_sc[...] = jnp.zeros(l_sc.shape, F32)
    acc_sc[...] = jnp.zeros(acc_sc.shape, F32)

    def attend(base, rows, sel, bias_of, bias_is_const):
        kc = k_ref[pl.ds(base, rows), :]
        vc = v_ref[pl.ds(base, rows), :]
        for n in range(N_KV_HEADS):
            kn = kc[:, n * HEAD_DIM:(n + 1) * HEAD_DIM]
            vn = vc[:, n * HEAD_DIM:(n + 1) * HEAD_DIM]
            for g in range(GROUP):
                h = n * GROUP + g
                lg = _dot_nt(kn, q_ref[:, h * HEAD_DIM:(h + 1) * HEAD_DIM])
                m_old = m_sc[h:h + 1, :]
                if bias_is_const:
                    lg = jnp.where(sel, lg, NEG_BIG)
                    m_new = jnp.maximum(m_old, jnp.max(lg, axis=0, keepdims=True) + bias_of(h))
                    p = jnp.exp(lg - (m_new - bias_of(h)))
                else:
                    lg = jnp.where(sel, lg + bias_of(h), NEG_BIG)
                    m_new = jnp.maximum(m_old, jnp.max(lg, axis=0, keepdims=True))
                    p = jnp.exp(lg - m_new)
                alpha = jnp.exp(m_old - m_new)
                l_sc[h:h + 1, :] = alpha * l_sc[h:h + 1, :] + jnp.sum(p, axis=0, keepdims=True)
                acc_sc[h] = alpha * acc_sc[h] + _dot_tn(vn, p.astype(BF16))
                m_sc[h:h + 1, :] = m_new

    far_hi = jnp.maximum(q_lo - QB, 0)
    n_far = (far_hi + lc - 1) // lc

    def far_chunk(c, carry):
        base = pl.multiple_of(c * lc, lc)
        sel = (key_sc[pl.ds(base, lc), :] >= thr) & (row_l + base < far_hi)
        attend(base, lc, sel, lambda h: cb_ref[h], True)
        return carry

    lax.fori_loop(0, n_far, far_chunk, 0)

    prev_lo = pl.multiple_of(jnp.maximum(q_lo - QB, 0), QB)
    sel_prev = (key_sc[pl.ds(prev_lo, QB), :] >= thr) & (i > 0)
    attend(prev_lo, QB, sel_prev, lambda h: pb_ref[h, :QB, :], False)
    diag_lo = pl.multiple_of(q_lo, QB)
    sel_diag = key_sc[pl.ds(diag_lo, QB), :] >= thr
    attend(diag_lo, QB, sel_diag, lambda h: pb_ref[h, QB:, :], False)

    for h in range(N_HEADS):
        o_t = acc_sc[h] / l_sc[h:h + 1, :]
        o_ref[:, h * HEAD_DIM:(h + 1) * HEAD_DIM] = o_t.T.astype(BF16)


def _prompt_attention(dm, q_b, qi_b, kw, ki_b, k_b, v_b, pb, cb):
    nq, tp = dm.nq, dm.tp
    topk = min(TOPK_MAX, dm.seq // 4)
    m = max(d for d in (1, 2, 3, 4) if nq % d == 0)
    lc = m * QB
    qblk = lambda w: pl.BlockSpec((QB, w), lambda b, i: (b * nq + i, 0))
    kblk = lambda w: pl.BlockSpec((tp, w), lambda b, i: (b, 0))
    qi_blk = pl.BlockSpec((IDX_HEADS, QB, IDX_DIM), lambda b, i: (0, b * nq + i, 0))
    return pl.pallas_call(
        functools.partial(_prompt_attn_kernel, topk=topk, lc=lc),
        grid=(dm.batch, nq),
        in_specs=[qi_blk, qblk(ATTN_WIDTH), qblk(LANES),
                  kblk(IDX_DIM), kblk(KV_WIDTH), kblk(KV_WIDTH),
                  _const_spec(pb.shape), _const_spec(cb.shape)],
        out_specs=qblk(ATTN_WIDTH),
        out_shape=jax.ShapeDtypeStruct((dm.prompt_rows, ATTN_WIDTH), BF16),
        scratch_shapes=[pltpu.VMEM((tp, QB), I32), pltpu.VMEM((N_HEADS, HEAD_DIM, QB), F32),
                        pltpu.VMEM((N_HEADS, QB), F32), pltpu.VMEM((N_HEADS, QB), F32)],
        compiler_params=_params(("arbitrary", "arbitrary")),
        name="prompt_attention",
    )(qi_b, q_b, kw, ki_b, k_b, v_b, pb, cb)


DEC_PAGES_MAX = 16


def _decode_kernel(pt_ref, qi_ref, w_ref, q_ref, kicur_ref, kcur_ref, vcur_ref, bfar_ref, bnear_ref, bcur_ref, dup_ref,
                   *refs, topk, n_pages, dec_seq, pps):
    ki_refs = refs[:pps]
    k_refs = refs[pps:2 * pps]
    v_refs = refs[2 * pps:3 * pps]
    o_ref, key_sc, thr_sc, m_sc, l_sc, acc_sc = refs[3 * pps:]
    n_groups = n_pages // pps
    s = pl.program_id(1)
    n_rows = N_KV_HEADS * GROUP * SUBLANES
    n_cols = N_KV_HEADS * PAGE_SIZE

    def scores(keys_t_bf):
        sc = _dot(qi_ref[0], keys_t_bf)
        acc = jnp.zeros((SUBLANES, LANES), F32)
        for h in range(IDX_HEADS):
            rows = slice(h * SUBLANES, (h + 1) * SUBLANES)
            acc = acc + w_ref[0, rows, :] * jnp.maximum(sc[rows, :], 0.0)
        return _order_key(acc)

    @pl.when(s < n_groups)
    def _():
        for j in range(pps):
            key_sc[s * pps + j] = scores(ki_refs[j][...].astype(BF16))

    @pl.when(s == n_groups - 1)
    def _():
        r = lax.broadcasted_iota(I32, (SUBLANES, LANES), 0)
        c = lax.broadcasted_iota(I32, (SUBLANES, LANES), 1)
        key_sc[n_pages] = jnp.where((c <= r) & (c < dec_seq), scores(kicur_ref[0]), INT_MIN)

        page_of = lax.broadcasted_iota(I32, (pps, SUBLANES, LANES), 0)
        lane_of = lax.broadcasted_iota(I32, (pps, SUBLANES, LANES), 2)

        def count_keys(pred):
            cnt = pred(key_sc[n_pages], c + n_pages * PAGE_SIZE).astype(I32)
            for blk in range(n_groups):
                pos = (page_of + blk * pps) * PAGE_SIZE + lane_of
                cnt = cnt + jnp.sum(pred(key_sc[pl.ds(blk * pps, pps)], pos).astype(I32), axis=0)
            return jnp.sum(cnt, axis=1, keepdims=True)

        def bisect(bit, thr):
            cand = thr + lax.shift_left(jnp.int32(1), 31 - bit)
            return jnp.where(count_keys(lambda k, l: k >= cand) >= topk, cand, thr)

        thr = lax.fori_loop(0, 32, bisect, jnp.full((SUBLANES, 1), INT_MIN, I32))

        surplus = count_keys(lambda k, l: k >= thr) - topk
        tied = (surplus > 0) & (thr > INT_MIN)

        @pl.when(jnp.max(tied.astype(I32)) > 0)
        def _():
            need = topk - count_keys(lambda k, l: k > thr)
            pos_bits = ((n_pages + 1) * PAGE_SIZE).bit_length()

            def bisect_pos(bit, cut):
                cand = cut + lax.shift_left(jnp.int32(1), pos_bits - 1 - bit)
                return jnp.where(count_keys(lambda k, l: (k == thr) & (l < cand)) < need, cand, cut)

            cut = lax.fori_loop(0, pos_bits, bisect_pos, jnp.zeros((SUBLANES, 1), I32))

            def drop_page(j, carry):
                k = key_sc[j]
                drop = tied & (k == thr) & (c + j * PAGE_SIZE > cut)
                key_sc[j] = jnp.where(drop, INT_MIN, k)
                return carry

            lax.fori_loop(0, n_pages + 1, drop_page, 0)

        thr_sc[...] = jnp.broadcast_to(jnp.maximum(thr, INT_MIN + 1), (SUBLANES, LANES))
        m_sc[...] = jnp.full(m_sc.shape, M_INIT, F32)
        l_sc[...] = jnp.zeros(l_sc.shape, F32)
        acc_sc[...] = jnp.zeros(acc_sc.shape, F32)

    def attend(key_tiles, k_of, v_of, bias_of):
        row_head = lax.broadcasted_iota(I32, (n_rows, n_cols), 0) // (GROUP * SUBLANES)
        col_head = lax.broadcasted_iota(I32, (n_rows, n_cols), 1) % N_KV_HEADS
        own_head = row_head == col_head
        thr = thr_sc[...]
        lgs = []
        for j, kt in enumerate(key_tiles):
            picked = jnp.where(kt >= thr, 1.0, 0.0).astype(BF16)
            picked = _dot(picked, dup_ref[...])
            picked = jnp.concatenate([picked] * (n_rows // SUBLANES), axis=0)
            lg = _dot_nt(q_ref[0], k_of(j)) + bias_of(j)
            lgs.append(jnp.where(jnp.where(own_head, picked, 0.0) > 0.5, lg, NEG_BIG))
        m_old = m_sc[...]
        m_new = jnp.maximum(m_old, jnp.max(functools.reduce(jnp.maximum, lgs), axis=1, keepdims=True))
        prs = [jnp.exp(lg - m_new) for lg in lgs]
        alpha = jnp.exp(m_old - m_new)
        l_sc[...] = alpha * l_sc[...] + jnp.sum(functools.reduce(jnp.add, prs), axis=1, keepdims=True)
        pv = functools.reduce(jnp.add, [_dot(pr.astype(BF16), v_of(j)) for j, pr in enumerate(prs)])
        acc_sc[...] = alpha * acc_sc[...] + pv
        m_sc[...] = m_new

    @pl.when(s >= n_groups)
    def _():
        g = s - n_groups
        last_group = g == n_groups - 1

        def bias_of(j):
            if j == pps - 1:
                return jnp.where(last_group, bnear_ref[...], bfar_ref[...])
            return bfar_ref[...]

        attend([key_sc[g * pps + j] for j in range(pps)],
               lambda j: k_refs[j][...].astype(BF16), lambda j: v_refs[j][...].astype(BF16), bias_of)

    @pl.when(s == 2 * n_groups - 1)
    def _():
        attend([key_sc[n_pages]], lambda j: kcur_ref[0], lambda j: vcur_ref[0], lambda j: bcur_ref[...])
        o_ref[0] = acc_sc[...] / l_sc[...]


def _decode_attention(dm, page_table, cache_k, cache_v, cache_kidx, qi_s, w_col, ki_cur, q_s, k_cur, v_cur,
                      bfar, bnear, bcur):
    db, np_ = dm.dec_batch, dm.n_pages
    topk = min(TOPK_MAX, (dm.past_len + dm.dec_seq) // 4)
    rows_hq = IDX_HEADS * SUBLANES
    n_rows = N_KV_HEADS * GROUP * SUBLANES
    n_cols = N_KV_HEADS * PAGE_SIZE
    pps = max(d for d in range(1, DEC_PAGES_MAX + 1) if np_ % d == 0)
    ng = np_ // pps
    ki_t = jnp.swapaxes(cache_kidx, 2, 3)
    k_rows = cache_k.reshape(cache_k.shape[:2] + (n_cols, HEAD_DIM))
    v_rows = cache_v.reshape(cache_v.shape[:2] + (n_cols, HEAD_DIM))
    dup = (jnp.arange(n_cols)[None, :] // N_KV_HEADS == jnp.arange(PAGE_SIZE)[:, None]).astype(BF16)
    per_b = lambda *shape: pl.BlockSpec((1,) + shape, lambda b, s, pt: (b,) + (0,) * len(shape))
    const = lambda shape: pl.BlockSpec(shape, lambda b, s, pt: (0,) * len(shape))
    ki_page = lambda j: pl.BlockSpec(
        (None, None, IDX_DIM, PAGE_SIZE), lambda b, s, pt: (0, pt[b, jnp.minimum(s, ng - 1) * pps + j], 0, 0))
    kv_page = lambda j: pl.BlockSpec(
        (None, None, n_cols, HEAD_DIM), lambda b, s, pt: (0, pt[b, jnp.maximum(s - ng, 0) * pps + j], 0, 0))
    pages = range(pps)
    return pl.pallas_call(
        functools.partial(_decode_kernel, topk=topk, n_pages=np_, dec_seq=dm.dec_seq, pps=pps),
        grid_spec=pltpu.PrefetchScalarGridSpec(
            num_scalar_prefetch=1, grid=(db, 2 * ng),
            in_specs=[per_b(rows_hq, IDX_DIM), per_b(rows_hq, 1), per_b(n_rows, HEAD_DIM),
                      per_b(IDX_DIM, PAGE_SIZE), per_b(n_cols, HEAD_DIM), per_b(n_cols, HEAD_DIM),
                      const(bfar.shape), const(bnear.shape), const(bcur.shape), const(dup.shape)]
                     + [ki_page(j) for j in pages] + [kv_page(j) for j in pages] + [kv_page(j) for j in pages],
            out_specs=per_b(n_rows, HEAD_DIM),
            scratch_shapes=[pltpu.VMEM((np_ + 1, SUBLANES, LANES), I32), pltpu.VMEM((SUBLANES, LANES), I32),
                            pltpu.VMEM((n_rows, 1), F32), pltpu.VMEM((n_rows, 1), F32),
                            pltpu.VMEM((n_rows, HEAD_DIM), F32)]),
        out_shape=jax.ShapeDtypeStruct((db, n_rows, HEAD_DIM), F32),
        compiler_params=_params(("arbitrary", "arbitrary")),
        name="decode_attention",
    )(page_table, qi_s, w_col, q_s, ki_cur, k_cur, v_cur, bfar, bnear, bcur, dup,
      *([ki_t] * pps), *([k_rows] * pps), *([v_rows] * pps))


CONV_CT = 256


def _conv_ln_swish(ext_ref, rows, w_ref, b_ref, g_ref, beta_ref, y_sc, out_ref_setter):
    off = CONV_HALO - (CONV_W - 1)
    for ct in range(CONV_CH // CONV_CT):
        cs = slice(ct * CONV_CT, (ct + 1) * CONV_CT)
        acc = jnp.zeros((rows, CONV_CT), F32)
        for j in range(CONV_W):
            acc = acc + ext_ref[pl.ds(off + j, rows), cs] * w_ref[j:j + 1, cs]
        y_sc[:, cs] = acc + b_ref[:, cs]
    y = y_sc[...]
    mu = jnp.mean(y, axis=-1, keepdims=True)
    d = y - mu
    var = jnp.mean(d * d, axis=-1, keepdims=True)
    yn = d * lax.rsqrt(var + EPS) * g_ref[...] + beta_ref[...]
    out_ref_setter(yn * jax.nn.sigmoid(yn))


def _prompt_conv_kernel(prev_ref, cur_ref, w_ref, b_ref, g_ref, beta_ref, c_ref, ext_sc, y_sc):
    i = pl.program_id(1)
    halo = prev_ref[QB - CONV_HALO:, :]
    ext_sc[:CONV_HALO, :] = jnp.where(i > 0, halo, 0.0)
    ext_sc[CONV_HALO:, :] = cur_ref[...]

    def put(c):
        c_ref[...] = c.astype(BF16)

    _conv_ln_swish(ext_sc, QB, w_ref, b_ref, g_ref, beta_ref, y_sc, put)


def _sample_conv_kernel(ext_ref, w_ref, b_ref, g_ref, beta_ref, c_ref, y_sc, *, rows):
    def put(c):
        c_ref[0] = c

    _conv_ln_swish(ext_ref.at[0], rows, w_ref, b_ref, g_ref, beta_ref, y_sc, put)


def _conv_weights(dw_w, dw_b, ln_g, ln_b):
    return (dw_w, dw_b.reshape(1, CONV_CH), ln_g.reshape(1, CONV_CH), ln_b.reshape(1, CONV_CH))


def _prompt_conv(dm, a_all, cw):
    nq = dm.nq
    blk = lambda f: pl.BlockSpec((QB, CONV_CH), f)
    return pl.pallas_call(
        _prompt_conv_kernel,
        grid=(dm.batch, nq),
        in_specs=[blk(lambda b, i: (jnp.maximum(b * nq + i - 1, 0), 0)), blk(lambda b, i: (b * nq + i, 0)),
                  _const_spec((CONV_W, CONV_CH))] + [_const_spec((1, CONV_CH))] * 3,
        out_specs=blk(lambda b, i: (b * nq + i, 0)),
        out_shape=jax.ShapeDtypeStruct((dm.prompt_rows, CONV_CH), BF16),
        scratch_shapes=[pltpu.VMEM((CONV_HALO + QB, CONV_CH), F32), pltpu.VMEM((QB, CONV_CH), F32)],
        compiler_params=_params(("arbitrary", "arbitrary")),
        name="prompt_conv",
    )(a_all, a_all, *cw)


def _sample_conv(dm, ext_s, cw):
    rows = dm.dec_seq
    tot = CONV_HALO + rows
    return pl.pallas_call(
        functools.partial(_sample_conv_kernel, rows=rows),
        grid=(dm.dec_batch,),
        in_specs=[pl.BlockSpec((1, tot, CONV_CH), lambda b: (b, 0, 0)), _const_spec((CONV_W, CONV_CH))]
                 + [_const_spec((1, CONV_CH))] * 3,
        out_specs=pl.BlockSpec((1, rows, CONV_CH), lambda b: (b, 0, 0)),
        out_shape=jax.ShapeDtypeStruct((dm.dec_batch, rows, CONV_CH), F32),
        scratch_shapes=[pltpu.VMEM((rows, CONV_CH), F32)],
        compiler_params=_params(("arbitrary",)),
        name="sample_conv",
    )(ext_s, *cw)


def _outproj_kernel(o_ref, c_ref, x_ref, wo_ref, wc_ref, g_ref, wpq_ref, h_ref, hn_ref, qp_ref):
    h = x_ref[...] + _dot(o_ref[...], wo_ref[...]) + _dot(c_ref[...], wc_ref[...])
    h_ref[...] = h
    ms = jnp.mean(h * h, axis=-1, keepdims=True)
    hn = h * lax.rsqrt(ms + EPS) * g_ref[...]
    hn_ref[...] = hn
    qp_ref[...] = _dot(hn.astype(BF16), wpq_ref[...]).astype(BF16)


def _out_projection(o_all, c_all, x_all, w_out, g_ffn, peer_wq):
    rows = x_all.shape[0]
    wo = w_out[:ATTN_WIDTH].astype(BF16)
    wc = w_out[ATTN_WIDTH:].astype(BF16)
    wpq = peer_wq.astype(BF16)
    row = lambda w: pl.BlockSpec((ROW_TILE, w), lambda i: (i, 0))
    pq = PEER_HEADS * PEER_KDIM
    return pl.pallas_call(
        _outproj_kernel,
        grid=(rows // ROW_TILE,),
        in_specs=[row(ATTN_WIDTH), row(CONV_CH), row(D_MODEL), _weight_spec(wo.shape), _weight_spec(wc.shape),
                  _const_spec((1, D_MODEL)), _weight_spec(wpq.shape)],
        out_specs=[row(D_MODEL), row(D_MODEL), row(pq)],
        out_shape=[jax.ShapeDtypeStruct((rows, D_MODEL), F32), jax.ShapeDtypeStruct((rows, D_MODEL), F32),
                   jax.ShapeDtypeStruct((rows, pq), BF16)],
        compiler_params=_params(("arbitrary",)),
        name="out_projection",
    )(o_all, c_all, x_all, wo, wc, g_ffn.reshape(1, D_MODEL), wpq)


def _extract_top(vals, pos, payload, count):
    big = jnp.int32(2 ** 30)
    out_v, out_p = [], []
    for _ in range(count):
        m = jnp.max(vals, axis=0, keepdims=True)
        first = jnp.min(jnp.where(vals == m, pos, big), axis=0, keepdims=True)
        hit = pos == first
        out_v.append(m)
        out_p.append(first if payload is None else jnp.sum(jnp.where(hit, payload, 0), axis=0, keepdims=True))
        vals = jnp.where(hit, -jnp.inf, vals)
    return jnp.concatenate(out_v, axis=0), jnp.concatenate(out_p, axis=0)


def _product_candidates(v1, i1, v2, i2):
    k = PEER_TOPK
    r8 = lax.broadcasted_iota(I32, (SUBLANES, LANES), 0)
    r16 = lax.broadcasted_iota(I32, (k, LANES), 0)
    vals = [v1[0:1] + v2]
    poss = [r16]
    idxs = [i1[0:1] * PEER_NKEYS + i2]
    for a in (1, 2, 3):
        vals.append(v1[a:a + 1] + v2[:SUBLANES])
        poss.append(a * k + r8)
        idxs.append(i1[a:a + 1] * PEER_NKEYS + i2[:SUBLANES])
    for b in (0, 1, 2):
        vals.append(jnp.where(r8 >= 4, v1[:SUBLANES] + v2[b:b + 1], -jnp.inf))
        poss.append(jnp.where(r8 >= 4, r8 * k + b, -1))
        idxs.append(i1[:SUBLANES] * PEER_NKEYS + i2[b:b + 1])
    vals.append(v1[SUBLANES:] + v2[0:1])
    poss.append((r8 + SUBLANES) * k)
    idxs.append(i1[SUBLANES:] * PEER_NKEYS + i2[0:1])
    cat = lambda xs: jnp.concatenate(xs, axis=0)
    return cat(vals), cat(poss), cat(idxs)


def _route_kernel(qp_ref, sk_ref, e_ref, g_ref):
    half = PEER_KDIM // 2
    key_id = lax.broadcasted_iota(I32, (PEER_NKEYS, LANES), 0)
    for h in range(PEER_HEADS):
        tops = []
        for c in range(2):
            qs = qp_ref[:, h * PEER_KDIM + c * half:h * PEER_KDIM + (c + 1) * half]
            s = _dot_nt(sk_ref[c], qs)
            tops.append(_extract_top(s, key_id, None, PEER_TOPK))
        (v1, i1), (v2, i2) = tops
        cand, cpos, cidx = _product_candidates(v1, i1, v2, i2)
        vals, experts = _extract_top(cand, cpos, cidx, PEER_TOPK)
        ex = jnp.exp(vals - vals[0:1, :])
        g_ref[h] = ex / jnp.sum(ex, axis=0, keepdims=True)
        e_ref[h] = experts


def _peer_route(qp, sub_keys):
    rows = qp.shape[0]
    sk = sub_keys.astype(BF16)
    out = pl.BlockSpec((PEER_HEADS, PEER_TOPK, LANES), lambda i: (0, 0, i))
    return pl.pallas_call(
        _route_kernel,
        grid=(rows // LANES,),
        in_specs=[pl.BlockSpec((LANES, PEER_HEADS * PEER_KDIM), lambda i: (i, 0)), _const_spec(sk.shape)],
        out_specs=[out, out],
        out_shape=[jax.ShapeDtypeStruct((PEER_HEADS, PEER_TOPK, rows), I32),
                   jax.ShapeDtypeStruct((PEER_HEADS, PEER_TOPK, rows), F32)],
        compiler_params=_params(("arbitrary",)),
        name="peer_route",
    )(qp, sk)


def _pack_bf16_pairs(t):
    bits = lax.bitcast_convert_type(t.astype(BF16), jnp.uint16).astype(jnp.uint32)
    half = t.shape[1] // 2
    return lax.bitcast_convert_type((bits[:, :half] << 16) | bits[:, half:], I32)


def _hi_half(w):
    return pltpu.bitcast(w & jnp.int32(-65536), F32)


def _lo_half(w):
    return pltpu.bitcast(lax.shift_left(w, jnp.int32(16)), F32)


def _peer_ffn_kernel(idx_ref, gate_ref, hn_ref, h_ref, uv_ref, y_ref, buf, out_sc, sem):
    n_groups = PEER_TOK_BLOCK // PEER_UNROLL
    half = D_MODEL // 2

    n_tiles = half // LANES
    per_tile = PEER_SLOTS // (2 * n_tiles)

    def issue(tok, slot, ks):
        for k in ks:
            pltpu.make_async_copy(uv_ref.at[idx_ref[tok, k]], buf.at[slot, pl.ds(k, 1)], sem.at[slot]).start(
                priority=k % PEER_DMA_QUEUES)

    def wait(slot):
        pltpu.make_async_copy(uv_ref.at[pl.ds(0, PEER_SLOTS), 0], buf.at[slot], sem.at[slot]).wait()

    for j in range(PEER_AHEAD):
        issue(j, j % PEER_BUFS, range(PEER_SLOTS))

    def group(gi, last):
        t0 = pl.multiple_of(gi * PEER_UNROLL, PEER_UNROLL)
        gates_t = gate_ref[pl.ds(t0, PEER_UNROLL), :].T
        x_grp = hn_ref[pl.ds(t0, PEER_UNROLL), :]
        for j in range(PEER_UNROLL):
            slot = j % PEER_BUFS
            ahead = j + PEER_AHEAD
            prefetch = ahead < PEER_UNROLL or not last

            def issue_part(part):
                if prefetch:
                    issue(t0 + ahead, ahead % PEER_BUFS, range(part * per_tile, (part + 1) * per_tile))

            wait(slot)
            hacc = jnp.zeros((PEER_SLOTS, LANES), F32)
            for c in range(n_tiles):
                issue_part(c)
                lo_cols = slice(c * LANES, (c + 1) * LANES)
                hi_cols = slice(half + c * LANES, half + (c + 1) * LANES)
                w = buf[slot, :, lo_cols]
                hacc = hacc + _hi_half(w) * x_grp[j:j + 1, lo_cols] + _lo_half(w) * x_grp[j:j + 1, hi_cols]
            hcol = jnp.sum(hacc, axis=1, keepdims=True)
            act = gates_t[:, j:j + 1] * jax.nn.gelu(hcol)
            act_b = jnp.broadcast_to(act, (PEER_SLOTS, LANES))
            for c in range(n_tiles):
                issue_part(n_tiles + c)
                lo_cols = slice(c * LANES, (c + 1) * LANES)
                hi_cols = slice(half + c * LANES, half + (c + 1) * LANES)
                w = buf[slot, :, half + c * LANES:half + (c + 1) * LANES]
                out_sc[j:j + 1, lo_cols] = jnp.sum(_hi_half(w) * act_b, axis=0, keepdims=True)
                out_sc[j:j + 1, hi_cols] = jnp.sum(_lo_half(w) * act_b, axis=0, keepdims=True)
        y_ref[pl.ds(t0, PEER_UNROLL), :] = h_ref[pl.ds(t0, PEER_UNROLL), :] + out_sc[...]

    def full_group(gi, carry):
        group(gi, last=False)
        return carry

    lax.fori_loop(0, n_groups - 1, full_group, 0)
    group(n_groups - 1, last=True)


def _peer_ffn(experts_tok, gates_tok, hn2, h, uv):
    rows = h.shape[0]
    tb = PEER_TOK_BLOCK
    row = lambda w: pl.BlockSpec((tb, w), lambda i: (i, 0))
    return pl.pallas_call(
        _peer_ffn_kernel,
        grid=(rows // tb,),
        in_specs=[pl.BlockSpec((tb, PEER_SLOTS), lambda i: (i, 0), memory_space=pltpu.SMEM),
                  row(PEER_SLOTS), row(D_MODEL), row(D_MODEL), pl.BlockSpec(memory_space=pl.ANY)],
        out_specs=row(D_MODEL),
        out_shape=jax.ShapeDtypeStruct((rows, D_MODEL), F32),
        scratch_shapes=[pltpu.VMEM((PEER_BUFS, PEER_SLOTS, D_MODEL), I32), pltpu.VMEM((PEER_UNROLL, D_MODEL), F32),
                        pltpu.SemaphoreType.DMA((PEER_BUFS,))],
        compiler_params=pltpu.CompilerParams(dimension_semantics=("arbitrary",), vmem_limit_bytes=VMEM_LIMIT,
                                             disable_bounds_checks=True),
        name="peer_ffn",
    )(experts_tok, gates_tok, hn2, h, uv)


def _step(dm, x_prompt, x_sample, cache_k, cache_v, cache_kidx, state_conv, page_table, meta_tokens, rel_bias,
          g_attn, w_in, g_q, g_k, conv_dw_w, conv_dw_b, conv_ln_g, conv_ln_b, w_out, g_ffn, peer_wq, peer_subkeys,
          peer_u, peer_v):
    bsz, t, tp, db, ds = dm.batch, dm.t_prompt, dm.tp, dm.dec_batch, dm.dec_seq
    pr, sr = dm.prompt_rows, dm.sample_rows

    meta = jnp.broadcast_to(meta_tokens[None].astype(x_prompt.dtype), (bsz, N_META, D_MODEL))
    xp = jnp.concatenate([meta, x_prompt, jnp.zeros((bsz, tp - t, D_MODEL), x_prompt.dtype)], axis=1)
    x_all = jnp.concatenate([xp.reshape(pr, D_MODEL), x_sample.reshape(sr, D_MODEL),
                             jnp.zeros((dm.rows - pr - sr, D_MODEL), x_prompt.dtype)], axis=0)

    q_b, k_f, v_f, k_b, v_b, qi_b, kw, ki_b, a_all = _in_projection(x_all, g_attn[0], w_in[0], g_q[0], g_k[0])

    pb, dn, dc = _bias_tiles(rel_bias)
    far = rel_bias[N_BUCKETS - 1]
    cb = jnp.broadcast_to(far[:, None, None], (N_HEADS, 1, QB))
    o_p = _prompt_attention(dm, q_b, qi_b, kw, ki_b, k_b, v_b, pb, cb)

    s0, s1 = pr, pr + sr
    qi_s = qi_b[:, s0:s1].reshape(IDX_HEADS, db, ds, IDX_DIM).transpose(1, 0, 2, 3).reshape(db, IDX_HEADS * ds, IDX_DIM)
    w_col = kw[s0:s1, IDX_DIM:IDX_DIM + IDX_HEADS].reshape(db, ds, IDX_HEADS).transpose(0, 2, 1)
    w_col = w_col.reshape(db, IDX_HEADS * ds, 1)
    q_s = q_b[s0:s1].reshape(db, ds, N_KV_HEADS, GROUP, HEAD_DIM).transpose(0, 2, 3, 1, 4)
    q_s = q_s.reshape(db, N_HEADS * ds, HEAD_DIM)
    pad_keys = lambda z: jnp.pad(z.reshape(db, ds, -1), ((0, 0), (0, PAGE_SIZE - ds), (0, 0)))
    kv_rows = lambda z: pad_keys(z).reshape(db, PAGE_SIZE * N_KV_HEADS, HEAD_DIM)
    by_rows = lambda z: jnp.repeat(z.reshape(N_HEADS * SUBLANES, LANES), N_KV_HEADS, axis=1)
    bfar = jnp.broadcast_to(far[:, None, None], (N_HEADS, SUBLANES, LANES))
    o_s = _decode_attention(
        dm, page_table, cache_k, cache_v, cache_kidx, qi_s, w_col, jnp.swapaxes(pad_keys(ki_b[s0:s1]), 1, 2), q_s,
        kv_rows(k_b[s0:s1]), kv_rows(v_b[s0:s1]), by_rows(bfar), by_rows(dn), by_rows(dc))
    o_s = o_s.reshape(db, N_HEADS, ds, HEAD_DIM).transpose(0, 2, 1, 3).reshape(sr, ATTN_WIDTH)

    cw = _conv_weights(conv_dw_w[0], conv_dw_b[0], conv_ln_g[0], conv_ln_b[0])
    c_p = _prompt_conv(dm, a_all, cw)
    a_s = a_all[s0:s1].reshape(db, ds, CONV_CH)
    ext_s = jnp.concatenate([jnp.zeros((db, CONV_HALO - (CONV_W - 1), CONV_CH), F32), state_conv[0], a_s], axis=1)
    c_s = _sample_conv(dm, ext_s, cw).reshape(sr, CONV_CH)

    tail = jnp.zeros((dm.rows - pr - sr, ATTN_WIDTH), BF16)
    o_all = jnp.concatenate([o_p, o_s.astype(BF16), tail], axis=0)
    c_all = jnp.concatenate([c_p, c_s.astype(BF16), tail], axis=0)
    h, hn2, qp = _out_projection(o_all, c_all, x_all, w_out[0], g_ffn[0], peer_wq[0])

    experts, gates = _peer_route(qp, peer_subkeys[0])
    experts_tok = experts.reshape(PEER_SLOTS, dm.rows).T
    gates_tok = gates.reshape(PEER_SLOTS, dm.rows).T
    uv = jnp.concatenate([_pack_bf16_pairs(peer_u[0]), _pack_bf16_pairs(peer_v[0])], axis=1)[:, None, :]
    y = _peer_ffn(experts_tok, gates_tok, hn2, h, uv)

    per_seq = lambda z, w: z[:pr].reshape(bsz, tp, w)[:, :t]
    y_prompt = per_seq(y, D_MODEL)[:, N_META:]
    y_sample = y[s0:s1].reshape(db, ds, D_MODEL)
    kv_shape = (N_KV_HEADS, HEAD_DIM)
    new_k_p = per_seq(k_f, KV_WIDTH).reshape(1, bsz, t, *kv_shape)
    new_v_p = per_seq(v_f, KV_WIDTH).reshape(1, bsz, t, *kv_shape)
    new_ki_p = per_seq(kw, LANES)[:, :, :IDX_DIM][None]
    new_conv_p = per_seq(a_all, CONV_CH)[:, t - (CONV_W - 1):][None]
    new_k_s = k_f[s0:s1].reshape(1, db, ds, *kv_shape)
    new_v_s = v_f[s0:s1].reshape(1, db, ds, *kv_shape)
    new_ki_s = kw[s0:s1, :IDX_DIM].reshape(1, db, ds, IDX_DIM)
    new_conv_s = ext_s[:, -(CONV_W - 1):][None]
    return (y_prompt, y_sample, new_k_p, new_v_p, new_ki_p, new_conv_p, new_k_s, new_v_s, new_ki_s, new_conv_s)


def kernel(x_prompt, x_sample, cache_k, cache_v, cache_kidx, state_conv, page_table, meta_tokens, rel_bias, g_attn,
           w_in, g_q, g_k, conv_dw_w, conv_dw_b, conv_ln_g, conv_ln_b, w_out, g_ffn, peer_wq, peer_subkeys, peer_u,
           peer_v):
    assert g_attn.shape[0] == 1, "single trunk layer"
    dm = Dims(batch=x_prompt.shape[0], seq=x_prompt.shape[1], dec_batch=x_sample.shape[0],
              dec_seq=x_sample.shape[1], past_len=page_table.shape[1] * PAGE_SIZE)
    assert dm.dec_seq == SUBLANES and dm.t_prompt >= CONV_W - 1
    return _step(dm, x_prompt, x_sample, cache_k, cache_v, cache_kidx, state_conv, page_table, meta_tokens, rel_bias,
                 g_attn, w_in, g_q, g_k, conv_dw_w, conv_dw_b, conv_ln_g, conv_ln_b, w_out, g_ffn, peer_wq,
                 peer_subkeys, peer_u, peer_v)
```

```python
import functools
import math
from typing import NamedTuple

import jax
import jax.numpy as jnp
from jax import lax
from jax.experimental import pallas as pl
from jax.experimental.pallas import tpu as pltpu

F32 = jnp.float32
BF16 = jnp.bfloat16
I32 = jnp.int32

D_MODEL = 2048
N_META = 16
N_HEADS = 8
HEAD_DIM = 128
N_KV_HEADS = 2
GROUP = N_HEADS // N_KV_HEADS
ATTN_WIDTH = N_HEADS * HEAD_DIM
KV_WIDTH = N_KV_HEADS * HEAD_DIM
CONV_CH = D_MODEL - ATTN_WIDTH
CONV_W = 31
IDX_HEADS = 16
IDX_DIM = 64
IDX_SCALE = (IDX_HEADS * IDX_DIM) ** -0.5
TOPK_MAX = 256
N_BUCKETS = 32
MAX_DISTANCE = 128
ATTN_SCALE = HEAD_DIM ** -0.5
PAGE_SIZE = 128
PEER_HEADS = 8
PEER_NKEYS = 128
PEER_KDIM = 128
PEER_TOPK = 16
PEER_SLOTS = PEER_HEADS * PEER_TOPK
EPS = 1e-6

LANES = 128
SUBLANES = 8
QB = 128
ROW_TILE = 256
CONV_HALO = 32
PEER_TOK_BLOCK = 256
PEER_UNROLL = 8
PEER_BUFS = 8
PEER_AHEAD = PEER_BUFS - 1
PEER_DMA_QUEUES = 2
INT_MIN = -(2 ** 31)
NEG_BIG = -1e30
M_INIT = -1e29
VMEM_LIMIT = 56 * 1024 * 1024

NT_DIMS = (((1,), (1,)), ((), ()))
TN_DIMS = (((0,), (0,)), ((), ()))


class Dims(NamedTuple):
    batch: int
    seq: int
    dec_batch: int
    dec_seq: int
    past_len: int

    @property
    def t_prompt(self):
        return self.seq + N_META

    @property
    def nq(self):
        return -(-self.t_prompt // QB)

    @property
    def tp(self):
        return self.nq * QB

    @property
    def prompt_rows(self):
        return self.batch * self.tp

    @property
    def sample_rows(self):
        return self.dec_batch * self.dec_seq

    @property
    def rows(self):
        r = self.prompt_rows + self.sample_rows
        return -(-r // ROW_TILE) * ROW_TILE

    @property
    def n_pages(self):
        return self.past_len // PAGE_SIZE


def _dot(a, b):
    return jnp.dot(a, b, preferred_element_type=F32)


def _dot_nt(a, b):
    return lax.dot_general(a, b, NT_DIMS, preferred_element_type=F32)


def _dot_tn(a, b):
    return lax.dot_general(a, b, TN_DIMS, preferred_element_type=F32)


def _params(sem):
    return pltpu.CompilerParams(dimension_semantics=sem, vmem_limit_bytes=VMEM_LIMIT)


def _const_spec(shape):
    nd = len(shape)
    return pl.BlockSpec(shape, lambda *_: (0,) * nd)


def _weight_spec(shape):
    nd = len(shape)
    return pl.BlockSpec(shape, lambda *_: (0,) * nd, pipeline_mode=pl.Buffered(1))


def _order_key(x):
    bits = pltpu.bitcast(x + 0.0, I32)
    return bits ^ ((bits >> 31) & 0x7FFFFFFF)


def _inproj_kernel(x_ref, ga_ref, wq_ref, wkv_ref, wqi_ref, wkw_ref, wu_ref, gq_ref, gk_ref,
                   q_ref, kf_ref, vf_ref, kb_ref, vb_ref, qi_ref, kw_ref, kib_ref, a_ref):
    x = x_ref[...]
    ms = jnp.mean(x * x, axis=-1, keepdims=True)
    hn = (x * lax.rsqrt(ms + EPS) * ga_ref[...]).astype(BF16)

    def head_norm(z, g):
        zm = jnp.mean(z * z, axis=-1, keepdims=True)
        return z * lax.rsqrt(zm + EPS) * g

    q = _dot(hn, wq_ref[...])
    for h in range(N_HEADS):
        sl = slice(h * HEAD_DIM, (h + 1) * HEAD_DIM)
        q_ref[:, sl] = (head_norm(q[:, sl], gq_ref[...]) * ATTN_SCALE).astype(BF16)
    kv = _dot(hn, wkv_ref[...])
    for n in range(N_KV_HEADS):
        sl = slice(n * HEAD_DIM, (n + 1) * HEAD_DIM)
        kn = head_norm(kv[:, sl], gk_ref[...])
        kf_ref[:, sl] = kn
        kb_ref[:, sl] = kn.astype(BF16)
    v = kv[:, KV_WIDTH:]
    vf_ref[...] = v
    vb_ref[...] = v.astype(BF16)
    qi = _dot(hn, wqi_ref[...]).astype(BF16)
    for h in range(IDX_HEADS):
        qi_ref[h] = qi[:, h * IDX_DIM:(h + 1) * IDX_DIM]
    kw = _dot(hn, wkw_ref[...])
    lane = lax.broadcasted_iota(I32, kw.shape, 1)
    kw = jnp.where(lane >= IDX_DIM, kw * IDX_SCALE, kw)
    kw_ref[...] = kw
    kib_ref[...] = kw[:, :IDX_DIM].astype(BF16)
    u = _dot(hn, wu_ref[...])
    a_ref[...] = u[:, :CONV_CH] * jax.nn.sigmoid(u[:, CONV_CH:])


def _in_projection(x_all, g_attn, w_in, g_q, g_k):
    rows = x_all.shape[0]
    o_q = ATTN_WIDTH
    o_k = o_q + KV_WIDTH
    o_v = o_k + KV_WIDTH
    o_qi = o_v + IDX_HEADS * IDX_DIM
    o_ki = o_qi + IDX_DIM
    o_wi = o_ki + IDX_HEADS
    wq = w_in[:, :o_q].astype(BF16)
    wkv = w_in[:, o_q:o_v].astype(BF16)
    wqi = w_in[:, o_v:o_qi].astype(BF16)
    wkw = jnp.pad(w_in[:, o_qi:o_wi], ((0, 0), (0, LANES - IDX_DIM - IDX_HEADS))).astype(BF16)
    wu = w_in[:, o_wi:].astype(BF16)
    row = lambda w: pl.BlockSpec((ROW_TILE, w), lambda i: (i, 0))
    outs = [
        (ATTN_WIDTH, BF16), (KV_WIDTH, F32), (KV_WIDTH, F32), (KV_WIDTH, BF16), (KV_WIDTH, BF16),
        None, (LANES, F32), (IDX_DIM, BF16), (CONV_CH, F32),
    ]
    qi_spec = pl.BlockSpec((IDX_HEADS, ROW_TILE, IDX_DIM), lambda i: (0, i, 0))
    qi_shape = jax.ShapeDtypeStruct((IDX_HEADS, rows, IDX_DIM), BF16)
    return pl.pallas_call(
        _inproj_kernel,
        grid=(rows // ROW_TILE,),
        in_specs=[row(D_MODEL), _const_spec((1, D_MODEL)), _weight_spec(wq.shape), _weight_spec(wkv.shape),
                  _weight_spec(wqi.shape), _weight_spec(wkw.shape), _weight_spec(wu.shape),
                  _const_spec((1, HEAD_DIM)), _const_spec((1, HEAD_DIM))],
        out_specs=[qi_spec if o is None else row(o[0]) for o in outs],
        out_shape=[qi_shape if o is None else jax.ShapeDtypeStruct((rows, o[0]), o[1]) for o in outs],
        compiler_params=_params(("arbitrary",)),
        name="in_projection",
    )(x_all, g_attn.reshape(1, D_MODEL), wq, wkv, wqi, wkw, wu, g_q.reshape(1, HEAD_DIM), g_k.reshape(1, HEAD_DIM))


def _t5_bias_of(dist, rb_ref, h):
    n = jnp.maximum(dist, 0)
    max_exact = N_BUCKETS // 2
    nf = jnp.maximum(n, 1).astype(F32)
    large = max_exact + (jnp.log(nf / max_exact) / math.log(MAX_DISTANCE / max_exact)
                         * (N_BUCKETS - max_exact)).astype(I32)
    large = jnp.minimum(large, N_BUCKETS - 1)
    bucket = jnp.where(n < max_exact, n, large)
    out = jnp.zeros(dist.shape, F32)
    for k in range(N_BUCKETS):
        out = jnp.where(bucket == k, rb_ref[k, h], out)
    return out


def _bias_kernel(rb_ref, pb_ref, dn_ref, dc_ref):
    r2 = lax.broadcasted_iota(I32, (2 * QB, QB), 0)
    c2 = lax.broadcasted_iota(I32, (2 * QB, QB), 1)
    r1 = lax.broadcasted_iota(I32, (SUBLANES, LANES), 0)
    c1 = lax.broadcasted_iota(I32, (SUBLANES, LANES), 1)
    for h in range(N_HEADS):
        pb_ref[h] = _t5_bias_of(QB + c2 - r2, rb_ref, h)
        dn_ref[h] = _t5_bias_of(PAGE_SIZE + r1 - c1, rb_ref, h)
        dc_ref[h] = _t5_bias_of(r1 - c1, rb_ref, h)


def _bias_tiles(rel_bias):
    return pl.pallas_call(
        _bias_kernel,
        in_specs=[pl.BlockSpec(memory_space=pltpu.SMEM)],
        out_shape=[jax.ShapeDtypeStruct((N_HEADS, 2 * QB, QB), F32),
                   jax.ShapeDtypeStruct((N_HEADS, SUBLANES, LANES), F32),
                   jax.ShapeDtypeStruct((N_HEADS, SUBLANES, LANES), F32)],
        name="t5_bias_tiles",
    )(rel_bias)


def _prompt_attn_kernel(qi_ref, q_ref, kw_ref, ki_ref, k_ref, v_ref, pb_ref, cb_ref, o_ref,
                        key_sc, acc_sc, m_sc, l_sc, *, topk, lc):
    i = pl.program_id(1)
    q_lo = i * QB
    n_chunks = (q_lo + QB + lc - 1) // lc
    w_t = kw_ref[...].T
    row_l = lax.broadcasted_iota(I32, (lc, QB), 0)
    qpos_l = lax.broadcasted_iota(I32, (lc, QB), 1) + q_lo

    def score_chunk(c, carry):
        base = pl.multiple_of(c * lc, lc)
        kc = ki_ref[pl.ds(base, lc), :]
        acc = jnp.zeros((lc, QB), F32)
        for hp in range(IDX_HEADS // 2):
            s2 = _dot_nt(kc, qi_ref[2 * hp:2 * hp + 2].reshape(2 * QB, IDX_DIM))
            for h in (2 * hp, 2 * hp + 1):
                s = s2[:, (h % 2) * QB:(h % 2 + 1) * QB]
                acc = acc + w_t[IDX_DIM + h:IDX_DIM + h + 1, :] * jnp.maximum(s, 0.0)
        key = jnp.where(row_l + base <= qpos_l, _order_key(acc), INT_MIN)
        key_sc[pl.ds(base, lc), :] = key
        return carry

    lax.fori_loop(0, n_chunks, score_chunk, 0)

    def count_keys(pred):
        def count_chunk(c, cnt):
            base = pl.multiple_of(c * lc, lc)
            hit = pred(key_sc[pl.ds(base, lc), :], row_l + base).astype(I32)
            return cnt + jnp.sum(hit.reshape(lc // SUBLANES, SUBLANES, QB), axis=0)

        cnt = lax.fori_loop(0, n_chunks, count_chunk, jnp.zeros((SUBLANES, QB), I32))
        return jnp.sum(cnt, axis=0, keepdims=True)

    def bisect(bit, thr):
        cand = thr + lax.shift_left(jnp.int32(1), 31 - bit)
        return jnp.where(count_keys(lambda k, l: k >= cand) >= topk, cand, thr)

    thr = lax.fori_loop(0, 32, bisect, jnp.full((1, QB), INT_MIN, I32))

    surplus = count_keys(lambda k, l: k >= thr) - topk
    tied = (surplus > 0) & (thr > INT_MIN)

    @pl.when(jnp.max(tied.astype(I32)) > 0)
    def _():
        need = topk - count_keys(lambda k, l: k > thr)
        pos_bits = key_sc.shape[0].bit_length()

        def bisect_pos(bit, cut):
            cand = cut + lax.shift_left(jnp.int32(1), pos_bits - 1 - bit)
            return jnp.where(count_keys(lambda k, l: (k == thr) & (l < cand)) < need, cand, cut)

        cut = lax.fori_loop(0, pos_bits, bisect_pos, jnp.zeros((1, QB), I32))

        def drop_chunk(c, carry):
            base = pl.multiple_of(c * lc, lc)
            k = key_sc[pl.ds(base, lc), :]
            drop = tied & (k == thr) & (row_l + base > cut)
            key_sc[pl.ds(base, lc), :] = jnp.where(drop, INT_MIN, k)
            return carry

        lax.fori_loop(0, n_chunks, drop_chunk, 0)

    thr = jnp.maximum(thr, INT_MIN + 1)

    m_sc[...] = jnp.full(m_sc.shape, M_INIT, F32)
    l_sc[...] = jnp.zeros(l_sc.shape, F32)
    acc_sc[...] = jnp.zeros(acc_sc.shape, F32)

    def attend(base, rows, sel, bias_of, bias_is_const):
        kc = k_ref[pl.ds(base, rows), :]
        vc = v_ref[pl.ds(base, rows), :]
        for n in range(N_KV_HEADS):
            kn = kc[:, n * HEAD_DIM:(n + 1) * HEAD_DIM]
            vn = vc[:, n * HEAD_DIM:(n + 1) * HEAD_DIM]
            for g in range(GROUP):
                h = n * GROUP + g
                lg = _dot_nt(kn, q_ref[:, h * HEAD_DIM:(h + 1) * HEAD_DIM])
                m_old = m_sc[h:h + 1, :]
                if bias_is_const:
                    lg = jnp.where(sel, lg, NEG_BIG)
                    m_new = jnp.maximum(m_old, jnp.max(lg, axis=0, keepdims=True) + bias_of(h))
                    p = jnp.exp(lg - (m_new - bias_of(h)))
                else:
                    lg = jnp.where(sel, lg + bias_of(h), NEG_BIG)
                    m_new = jnp.maximum(m_old, jnp.max(lg, axis=0, keepdims=True))
                    p = jnp.exp(lg - m_new)
                alpha = jnp.exp(m_old - m_new)
                l_sc[h:h + 1, :] = alpha * l_sc[h:h + 1, :] + jnp.sum(p, axis=0, keepdims=True)
                acc_sc[h] = alpha * acc_sc[h] + _dot_tn(vn, p.astype(BF16))
                m_sc[h:h + 1, :] = m_new

    far_hi = jnp.maximum(q_lo - QB, 0)
    n_far = (far_hi + lc - 1) // lc

    def far_chunk(c, carry):
        base = pl.multiple_of(c * lc, lc)
        sel = (key_sc[pl.ds(base, lc), :] >= thr) & (row_l + base < far_hi)
        attend(base, lc, sel, lambda h: cb_ref[h], True)
        return carry

    lax.fori_loop(0, n_far, far_chunk, 0)

    prev_lo = pl.multiple_of(jnp.maximum(q_lo - QB, 0), QB)
    sel_prev = (key_sc[pl.ds(prev_lo, QB), :] >= thr) & (i > 0)
    attend(prev_lo, QB, sel_prev, lambda h: pb_ref[h, :QB, :], False)
    diag_lo = pl.multiple_of(q_lo, QB)
    sel_diag = key_sc[pl.ds(diag_lo, QB), :] >= thr
    attend(diag_lo, QB, sel_diag, lambda h: pb_ref[h, QB:, :], False)

    for h in range(N_HEADS):
        o_t = acc_sc[h] / l_sc[h:h + 1, :]
        o_ref[:, h * HEAD_DIM:(h + 1) * HEAD_DIM] = o_t.T.astype(BF16)


def _prompt_attention(dm, q_b, qi_b, kw, ki_b, k_b, v_b, pb, cb):
    nq, tp = dm.nq, dm.tp
    topk = min(TOPK_MAX, dm.seq // 4)
    m = max(d for d in (1, 2, 3, 4) if nq % d == 0)
    lc = m * QB
    qblk = lambda w: pl.BlockSpec((QB, w), lambda b, i: (b * nq + i, 0))
    kblk = lambda w: pl.BlockSpec((tp, w), lambda b, i: (b, 0))
    qi_blk = pl.BlockSpec((IDX_HEADS, QB, IDX_DIM), lambda b, i: (0, b * nq + i, 0))
    return pl.pallas_call(
        functools.partial(_prompt_attn_kernel, topk=topk, lc=lc),
        grid=(dm.batch, nq),
        in_specs=[qi_blk, qblk(ATTN_WIDTH), qblk(LANES),
                  kblk(IDX_DIM), kblk(KV_WIDTH), kblk(KV_WIDTH),
                  _const_spec(pb.shape), _const_spec(cb.shape)],
        out_specs=qblk(ATTN_WIDTH),
        out_shape=jax.ShapeDtypeStruct((dm.prompt_rows, ATTN_WIDTH), BF16),
        scratch_shapes=[pltpu.VMEM((tp, QB), I32), pltpu.VMEM((N_HEADS, HEAD_DIM, QB), F32),
                        pltpu.VMEM((N_HEADS, QB), F32), pltpu.VMEM((N_HEADS, QB), F32)],
        compiler_params=_params(("arbitrary", "arbitrary")),
        name="prompt_attention",
    )(qi_b, q_b, kw, ki_b, k_b, v_b, pb, cb)


DEC_PAGES_MAX = 16


def _decode_kernel(pt_ref, qi_ref, w_ref, q_ref, kicur_ref, kcur_ref, vcur_ref, bfar_ref, bnear_ref, bcur_ref, dup_ref,
                   *refs, topk, n_pages, dec_seq, pps):
    ki_refs = refs[:pps]
    k_refs = refs[pps:2 * pps]
    v_refs = refs[2 * pps:3 * pps]
    o_ref, key_sc, thr_sc, m_sc, l_sc, acc_sc = refs[3 * pps:]
    n_groups = n_pages // pps
    s = pl.program_id(1)
    n_rows = N_KV_HEADS * GROUP * SUBLANES
    n_cols = N_KV_HEADS * PAGE_SIZE

    def scores(keys_t_bf):
        sc = _dot(qi_ref[0], keys_t_bf)
        acc = jnp.zeros((SUBLANES, LANES), F32)
        for h in range(IDX_HEADS):
            rows = slice(h * SUBLANES, (h + 1) * SUBLANES)
            acc = acc + w_ref[0, rows, :] * jnp.maximum(sc[rows, :], 0.0)
        return _order_key(acc)

    @pl.when(s < n_groups)
    def _():
        for j in range(pps):
            key_sc[s * pps + j] = scores(ki_refs[j][...].astype(BF16))

    @pl.when(s == n_groups - 1)
    def _():
        r = lax.broadcasted_iota(I32, (SUBLANES, LANES), 0)
        c = lax.broadcasted_iota(I32, (SUBLANES, LANES), 1)
        key_sc[n_pages] = jnp.where((c <= r) & (c < dec_seq), scores(kicur_ref[0]), INT_MIN)

        page_of = lax.broadcasted_iota(I32, (pps, SUBLANES, LANES), 0)
        lane_of = lax.broadcasted_iota(I32, (pps, SUBLANES, LANES), 2)

        def count_keys(pred):
            cnt = pred(key_sc[n_pages], c + n_pages * PAGE_SIZE).astype(I32)
            for blk in range(n_groups):
                pos = (page_of + blk * pps) * PAGE_SIZE + lane_of
                cnt = cnt + jnp.sum(pred(key_sc[pl.ds(blk * pps, pps)], pos).astype(I32), axis=0)
            return jnp.sum(cnt, axis=1, keepdims=True)

        def bisect(bit, thr):
            cand = thr + lax.shift_left(jnp.int32(1), 31 - bit)
            return jnp.where(count_keys(lambda k, l: k >= cand) >= topk, cand, thr)

        thr = lax.fori_loop(0, 32, bisect, jnp.full((SUBLANES, 1), INT_MIN, I32))

        surplus = count_keys(lambda k, l: k >= thr) - topk
        tied = (surplus > 0) & (thr > INT_MIN)

        @pl.when(jnp.max(tied.astype(I32)) > 0)
        def _():
            need = topk - count_keys(lambda k, l: k > thr)
            pos_bits = ((n_pages + 1) * PAGE_SIZE).bit_length()

            def bisect_pos(bit, cut):
                cand = cut + lax.shift_left(jnp.int32(1), pos_bits - 1 - bit)
                return jnp.where(count_keys(lambda k, l: (k == thr) & (l < cand)) < need, cand, cut)

            cut = lax.fori_loop(0, pos_bits, bisect_pos, jnp.zeros((SUBLANES, 1), I32))

            def drop_page(j, carry):
                k = key_sc[j]
                drop = tied & (k == thr) & (c + j * PAGE_SIZE > cut)
                key_sc[j] = jnp.where(drop, INT_MIN, k)
                return carry

            lax.fori_loop(0, n_pages + 1, drop_page, 0)

        thr_sc[...] = jnp.broadcast_to(jnp.maximum(thr, INT_MIN + 1), (SUBLANES, LANES))
        m_sc[...] = jnp.full(m_sc.shape, M_INIT, F32)
        l_sc[...] = jnp.zeros(l_sc.shape, F32)
        acc_sc[...] = jnp.zeros(acc_sc.shape, F32)

    def attend(key_tiles, k_of, v_of, bias_of):
        row_head = lax.broadcasted_iota(I32, (n_rows, n_cols), 0) // (GROUP * SUBLANES)
        col_head = lax.broadcasted_iota(I32, (n_rows, n_cols), 1) % N_KV_HEADS
        own_head = row_head == col_head
        thr = thr_sc[...]
        lgs = []
        for j, kt in enumerate(key_tiles):
            picked = jnp.where(kt >= thr, 1.0, 0.0).astype(BF16)
            picked = _dot(picked, dup_ref[...])
            picked = jnp.concatenate([picked] * (n_rows // SUBLANES), axis=0)
            lg = _dot_nt(q_ref[0], k_of(j)) + bias_of(j)
            lgs.append(jnp.where(jnp.where(own_head, picked, 0.0) > 0.5, lg, NEG_BIG))
        m_old = m_sc[...]
        m_new = jnp.maximum(m_old, jnp.max(functools.reduce(jnp.maximum, lgs), axis=1, keepdims=True))
        prs = [jnp.exp(lg - m_new) for lg in lgs]
        alpha = jnp.exp(m_old - m_new)
        l_sc[...] = alpha * l_sc[...] + jnp.sum(functools.reduce(jnp.add, prs), axis=1, keepdims=True)
        pv = functools.reduce(jnp.add, [_dot(pr.astype(BF16), v_of(j)) for j, pr in enumerate(prs)])
        acc_sc[...] = alpha * acc_sc[...] + pv
        m_sc[...] = m_new

    @pl.when(s >= n_groups)
    def _():
        g = s - n_groups
        last_group = g == n_groups - 1

        def bias_of(j):
            if j == pps - 1:
                return jnp.where(last_group, bnear_ref[...], bfar_ref[...])
            return bfar_ref[...]

        attend([key_sc[g * pps + j] for j in range(pps)],
               lambda j: k_refs[j][...].astype(BF16), lambda j: v_refs[j][...].astype(BF16), bias_of)

    @pl.when(s == 2 * n_groups - 1)
    def _():
        attend([key_sc[n_pages]], lambda j: kcur_ref[0], lambda j: vcur_ref[0], lambda j: bcur_ref[...])
        o_ref[0] = acc_sc[...] / l_sc[...]


def _decode_attention(dm, page_table, cache_k, cache_v, cache_kidx, qi_s, w_col, ki_cur, q_s, k_cur, v_cur,
                      bfar, bnear, bcur):
    db, np_ = dm.dec_batch, dm.n_pages
    topk = min(TOPK_MAX, (dm.past_len + dm.dec_seq) // 4)
    rows_hq = IDX_HEADS * SUBLANES
    n_rows = N_KV_HEADS * GROUP * SUBLANES
    n_cols = N_KV_HEADS * PAGE_SIZE
    pps = max(d for d in range(1, DEC_PAGES_MAX + 1) if np_ % d == 0)
    ng = np_ // pps
    ki_t = jnp.swapaxes(cache_kidx, 2, 3)
    k_rows = cache_k.reshape(cache_k.shape[:2] + (n_cols, HEAD_DIM))
    v_rows = cache_v.reshape(cache_v.shape[:2] + (n_cols, HEAD_DIM))
    dup = (jnp.arange(n_cols)[None, :] // N_KV_HEADS == jnp.arange(PAGE_SIZE)[:, None]).astype(BF16)
    per_b = lambda *shape: pl.BlockSpec((1,) + shape, lambda b, s, pt: (b,) + (0,) * len(shape))
    const = lambda shape: pl.BlockSpec(shape, lambda b, s, pt: (0,) * len(shape))
    ki_page = lambda j: pl.BlockSpec(
        (None, None, IDX_DIM, PAGE_SIZE), lambda b, s, pt: (0, pt[b, jnp.minimum(s, ng - 1) * pps + j], 0, 0))
    kv_page = lambda j: pl.BlockSpec(
        (None, None, n_cols, HEAD_DIM), lambda b, s, pt: (0, pt[b, jnp.maximum(s - ng, 0) * pps + j], 0, 0))
    pages = range(pps)
    return pl.pallas_call(
        functools.partial(_decode_kernel, topk=topk, n_pages=np_, dec_seq=dm.dec_seq, pps=pps),
        grid_spec=pltpu.PrefetchScalarGridSpec(
            num_scalar_prefetch=1, grid=(db, 2 * ng),
            in_specs=[per_b(rows_hq, IDX_DIM), per_b(rows_hq, 1), per_b(n_rows, HEAD_DIM),
                      per_b(IDX_DIM, PAGE_SIZE), per_b(n_cols, HEAD_DIM), per_b(n_cols, HEAD_DIM),
                      const(bfar.shape), const(bnear.shape), const(bcur.shape), const(dup.shape)]
                     + [ki_page(j) for j in pages] + [kv_page(j) for j in pages] + [kv_page(j) for j in pages],
            out_specs=per_b(n_rows, HEAD_DIM),
            scratch_shapes=[pltpu.VMEM((np_ + 1, SUBLANES, LANES), I32), pltpu.VMEM((SUBLANES, LANES), I32),
                            pltpu.VMEM((n_rows, 1), F32), pltpu.VMEM((n_rows, 1), F32),
                            pltpu.VMEM((n_rows, HEAD_DIM), F32)]),
        out_shape=jax.ShapeDtypeStruct((db, n_rows, HEAD_DIM), F32),
        compiler_params=_params(("arbitrary", "arbitrary")),
        name="decode_attention",
    )(page_table, qi_s, w_col, q_s, ki_cur, k_cur, v_cur, bfar, bnear, bcur, dup,
      *([ki_t] * pps), *([k_rows] * pps), *([v_rows] * pps))


CONV_CT = 256


def _conv_ln_swish(ext_ref, rows, w_ref, b_ref, g_ref, beta_ref, y_sc, out_ref_setter):
    off = CONV_HALO - (CONV_W - 1)
    for ct in range(CONV_CH // CONV_CT):
        cs = slice(ct * CONV_CT, (ct + 1) * CONV_CT)
        acc = jnp.zeros((rows, CONV_CT), F32)
        for j in range(CONV_W):
            acc = acc + ext_ref[pl.ds(off + j, rows), cs] * w_ref[j:j + 1, cs]
        y_sc[:, cs] = acc + b_ref[:, cs]
    y = y_sc[...]
    mu = jnp.mean(y, axis=-1, keepdims=True)
    d = y - mu
    var = jnp.mean(d * d, axis=-1, keepdims=True)
    yn = d * lax.rsqrt(var + EPS) * g_ref[...] + beta_ref[...]
    out_ref_setter(yn * jax.nn.sigmoid(yn))


def _prompt_conv_kernel(prev_ref, cur_ref, w_ref, b_ref, g_ref, beta_ref, c_ref, ext_sc, y_sc):
    i = pl.program_id(1)
    halo = prev_ref[QB - CONV_HALO:, :]
    ext_sc[:CONV_HALO, :] = jnp.where(i > 0, halo, 0.0)
    ext_sc[CONV_HALO:, :] = cur_ref[...]

    def put(c):
        c_ref[...] = c.astype(BF16)

    _conv_ln_swish(ext_sc, QB, w_ref, b_ref, g_ref, beta_ref, y_sc, put)


def _sample_conv_kernel(ext_ref, w_ref, b_ref, g_ref, beta_ref, c_ref, y_sc, *, rows):
    def put(c):
        c_ref[0] = c

    _conv_ln_swish(ext_ref.at[0], rows, w_ref, b_ref, g_ref, beta_ref, y_sc, put)


def _conv_weights(dw_w, dw_b, ln_g, ln_b):
    return (dw_w, dw_b.reshape(1, CONV_CH), ln_g.reshape(1, CONV_CH), ln_b.reshape(1, CONV_CH))


def _prompt_conv(dm, a_all, cw):
    nq = dm.nq
    blk = lambda f: pl.BlockSpec((QB, CONV_CH), f)
    return pl.pallas_call(
        _prompt_conv_kernel,
        grid=(dm.batch, nq),
        in_specs=[blk(lambda b, i: (jnp.maximum(b * nq + i - 1, 0), 0)), blk(lambda b, i: (b * nq + i, 0)),
                  _const_spec((CONV_W, CONV_CH))] + [_const_spec((1, CONV_CH))] * 3,
        out_specs=blk(lambda b, i: (b * nq + i, 0)),
        out_shape=jax.ShapeDtypeStruct((dm.prompt_rows, CONV_CH), BF16),
        scratch_shapes=[pltpu.VMEM((CONV_HALO + QB, CONV_CH), F32), pltpu.VMEM((QB, CONV_CH), F32)],
        compiler_params=_params(("arbitrary", "arbitrary")),
        name="prompt_conv",
    )(a_all, a_all, *cw)


def _sample_conv(dm, ext_s, cw):
    rows = dm.dec_seq
    tot = CONV_HALO + rows
    return pl.pallas_call(
        functools.partial(_sample_conv_kernel, rows=rows),
        grid=(dm.dec_batch,),
        in_specs=[pl.BlockSpec((1, tot, CONV_CH), lambda b: (b, 0, 0)), _const_spec((CONV_W, CONV_CH))]
                 + [_const_spec((1, CONV_CH))] * 3,
        out_specs=pl.BlockSpec((1, rows, CONV_CH), lambda b: (b, 0, 0)),
        out_shape=jax.ShapeDtypeStruct((dm.dec_batch, rows, CONV_CH), F32),
        scratch_shapes=[pltpu.VMEM((rows, CONV_CH), F32)],
        compiler_params=_params(("arbitrary",)),
        name="sample_conv",
    )(ext_s, *cw)


def _outproj_kernel(o_ref, c_ref, x_ref, wo_ref, wc_ref, g_ref, wpq_ref, h_ref, hn_ref, qp_ref):
    h = x_ref[...] + _dot(o_ref[...], wo_ref[...]) + _dot(c_ref[...], wc_ref[...])
    h_ref[...] = h
    ms = jnp.mean(h * h, axis=-1, keepdims=True)
    hn = h * lax.rsqrt(ms + EPS) * g_ref[...]
    hn_ref[...] = hn
    qp_ref[...] = _dot(hn.astype(BF16), wpq_ref[...]).astype(BF16)


def _out_projection(o_all, c_all, x_all, w_out, g_ffn, peer_wq):
    rows = x_all.shape[0]
    wo = w_out[:ATTN_WIDTH].astype(BF16)
    wc = w_out[ATTN_WIDTH:].astype(BF16)
    wpq = peer_wq.astype(BF16)
    row = lambda w: pl.BlockSpec((ROW_TILE, w), lambda i: (i, 0))
    pq = PEER_HEADS * PEER_KDIM
    return pl.pallas_call(
        _outproj_kernel,
        grid=(rows // ROW_TILE,),
        in_specs=[row(ATTN_WIDTH), row(CONV_CH), row(D_MODEL), _weight_spec(wo.shape), _weight_spec(wc.shape),
                  _const_spec((1, D_MODEL)), _weight_spec(wpq.shape)],
        out_specs=[row(D_MODEL), row(D_MODEL), row(pq)],
        out_shape=[jax.ShapeDtypeStruct((rows, D_MODEL), F32), jax.ShapeDtypeStruct((rows, D_MODEL), F32),
                   jax.ShapeDtypeStruct((rows, pq), BF16)],
        compiler_params=_params(("arbitrary",)),
        name="out_projection",
    )(o_all, c_all, x_all, wo, wc, g_ffn.reshape(1, D_MODEL), wpq)


def _extract_top(vals, pos, payload, count):
    big = jnp.int32(2 ** 30)
    out_v, out_p = [], []
    for _ in range(count):
        m = jnp.max(vals, axis=0, keepdims=True)
        first = jnp.min(jnp.where(vals == m, pos, big), axis=0, keepdims=True)
        hit = pos == first
        out_v.append(m)
        out_p.append(first if payload is None else jnp.sum(jnp.where(hit, payload, 0), axis=0, keepdims=True))
        vals = jnp.where(hit, -jnp.inf, vals)
    return jnp.concatenate(out_v, axis=0), jnp.concatenate(out_p, axis=0)


def _product_candidates(v1, i1, v2, i2):
    k = PEER_TOPK
    r8 = lax.broadcasted_iota(I32, (SUBLANES, LANES), 0)
    r16 = lax.broadcasted_iota(I32, (k, LANES), 0)
    vals = [v1[0:1] + v2]
    poss = [r16]
    idxs = [i1[0:1] * PEER_NKEYS + i2]
    for a in (1, 2, 3):
        vals.append(v1[a:a + 1] + v2[:SUBLANES])
        poss.append(a * k + r8)
        idxs.append(i1[a:a + 1] * PEER_NKEYS + i2[:SUBLANES])
    for b in (0, 1, 2):
        vals.append(jnp.where(r8 >= 4, v1[:SUBLANES] + v2[b:b + 1], -jnp.inf))
        poss.append(jnp.where(r8 >= 4, r8 * k + b, -1))
        idxs.append(i1[:SUBLANES] * PEER_NKEYS + i2[b:b + 1])
    vals.append(v1[SUBLANES:] + v2[0:1])
    poss.append((r8 + SUBLANES) * k)
    idxs.append(i1[SUBLANES:] * PEER_NKEYS + i2[0:1])
    cat = lambda xs: jnp.concatenate(xs, axis=0)
    return cat(vals), cat(poss), cat(idxs)


def _route_kernel(qp_ref, sk_ref, e_ref, g_ref):
    half = PEER_KDIM // 2
    key_id = lax.broadcasted_iota(I32, (PEER_NKEYS, LANES), 0)
    for h in range(PEER_HEADS):
        tops = []
        for c in range(2):
            qs = qp_ref[:, h * PEER_KDIM + c * half:h * PEER_KDIM + (c + 1) * half]
            s = _dot_nt(sk_ref[c], qs)
            tops.append(_extract_top(s, key_id, None, PEER_TOPK))
        (v1, i1), (v2, i2) = tops
        cand, cpos, cidx = _product_candidates(v1, i1, v2, i2)
        vals, experts = _extract_top(cand, cpos, cidx, PEER_TOPK)
        ex = jnp.exp(vals - vals[0:1, :])
        g_ref[h] = ex / jnp.sum(ex, axis=0, keepdims=True)
        e_ref[h] = experts


def _peer_route(qp, sub_keys):
    rows = qp.shape[0]
    sk = sub_keys.astype(BF16)
    out = pl.BlockSpec((PEER_HEADS, PEER_TOPK, LANES), lambda i: (0, 0, i))
    return pl.pallas_call(
        _route_kernel,
        grid=(rows // LANES,),
        in_specs=[pl.BlockSpec((LANES, PEER_HEADS * PEER_KDIM), lambda i: (i, 0)), _const_spec(sk.shape)],
        out_specs=[out, out],
        out_shape=[jax.ShapeDtypeStruct((PEER_HEADS, PEER_TOPK, rows), I32),
                   jax.ShapeDtypeStruct((PEER_HEADS, PEER_TOPK, rows), F32)],
        compiler_params=_params(("arbitrary",)),
        name="peer_route",
    )(qp, sk)


def _pack_bf16_pairs(t):
    half = t.shape[1] // 2
    bits = lambda z: lax.bitcast_convert_type(z.astype(BF16), jnp.uint16).astype(jnp.uint32)
    return lax.bitcast_convert_type((bits(t[:, :half]) << 16) | bits(t[:, half:]), I32)


def _hi_half(w):
    return pltpu.bitcast(w & jnp.int32(-65536), F32)


def _lo_half(w):
    return pltpu.bitcast(lax.shift_left(w, jnp.int32(16)), F32)


def _peer_ffn_kernel(idx_ref, gate_ref, hn_ref, h_ref, uv_ref, y_ref, buf, out_sc, sem):
    n_groups = PEER_TOK_BLOCK // PEER_UNROLL
    half = D_MODEL // 2

    n_tiles = half // LANES
    per_tile = PEER_SLOTS // (2 * n_tiles)

    def issue(tok, slot, ks):
        for k in ks:
            pltpu.make_async_copy(uv_ref.at[idx_ref[tok, k]], buf.at[slot, pl.ds(k, 1)], sem.at[slot]).start(
                priority=k % PEER_DMA_QUEUES)

    def wait(slot):
        pltpu.make_async_copy(uv_ref.at[pl.ds(0, PEER_SLOTS), 0], buf.at[slot], sem.at[slot]).wait()

    for j in range(PEER_AHEAD):
        issue(j, j % PEER_BUFS, range(PEER_SLOTS))

    def group(gi, last):
        t0 = pl.multiple_of(gi * PEER_UNROLL, PEER_UNROLL)
        gates_t = gate_ref[pl.ds(t0, PEER_UNROLL), :].T
        x_grp = hn_ref[pl.ds(t0, PEER_UNROLL), :]
        for j in range(PEER_UNROLL):
            slot = j % PEER_BUFS
            ahead = j + PEER_AHEAD
            prefetch = ahead < PEER_UNROLL or not last

            def issue_part(part):
                if prefetch:
                    issue(t0 + ahead, ahead % PEER_BUFS, range(part * per_tile, (part + 1) * per_tile))

            wait(slot)
            hacc = jnp.zeros((PEER_SLOTS, LANES), F32)
            for c in range(n_tiles):
                issue_part(c)
                lo_cols = slice(c * LANES, (c + 1) * LANES)
                hi_cols = slice(half + c * LANES, half + (c + 1) * LANES)
                w = buf[slot, :, lo_cols]
                hacc = hacc + _hi_half(w) * x_grp[j:j + 1, lo_cols] + _lo_half(w) * x_grp[j:j + 1, hi_cols]
            hcol = jnp.sum(hacc, axis=1, keepdims=True)
            act = gates_t[:, j:j + 1] * jax.nn.gelu(hcol)
            act_b = jnp.broadcast_to(act, (PEER_SLOTS, LANES))
            for c in range(n_tiles):
                issue_part(n_tiles + c)
                lo_cols = slice(c * LANES, (c + 1) * LANES)
                hi_cols = slice(half + c * LANES, half + (c + 1) * LANES)
                w = buf[slot, :, half + c * LANES:half + (c + 1) * LANES]
                out_sc[j:j + 1, lo_cols] = jnp.sum(_hi_half(w) * act_b, axis=0, keepdims=True)
                out_sc[j:j + 1, hi_cols] = jnp.sum(_lo_half(w) * act_b, axis=0, keepdims=True)
        y_ref[pl.ds(t0, PEER_UNROLL), :] = h_ref[pl.ds(t0, PEER_UNROLL), :] + out_sc[...]

    def full_group(gi, carry):
        group(gi, last=False)
        return carry

    lax.fori_loop(0, n_groups - 1, full_group, 0)
    group(n_groups - 1, last=True)


def _peer_ffn(experts_tok, gates_tok, hn2, h, uv):
    rows = h.shape[0]
    tb = PEER_TOK_BLOCK
    row = lambda w: pl.BlockSpec((tb, w), lambda i: (i, 0))
    return pl.pallas_call(
        _peer_ffn_kernel,
        grid=(rows // tb,),
        in_specs=[pl.BlockSpec((tb, PEER_SLOTS), lambda i: (i, 0), memory_space=pltpu.SMEM),
                  row(PEER_SLOTS), row(D_MODEL), row(D_MODEL), pl.BlockSpec(memory_space=pl.ANY)],
        out_specs=row(D_MODEL),
        out_shape=jax.ShapeDtypeStruct((rows, D_MODEL), F32),
        scratch_shapes=[pltpu.VMEM((PEER_BUFS, PEER_SLOTS, D_MODEL), I32), pltpu.VMEM((PEER_UNROLL, D_MODEL), F32),
                        pltpu.SemaphoreType.DMA((PEER_BUFS,))],
        compiler_params=pltpu.CompilerParams(dimension_semantics=("arbitrary",), vmem_limit_bytes=VMEM_LIMIT,
                                             disable_bounds_checks=True),
        name="peer_ffn",
    )(experts_tok, gates_tok, hn2, h, uv)


def _step(dm, x_prompt, x_sample, cache_k, cache_v, cache_kidx, state_conv, page_table, meta_tokens, rel_bias,
          g_attn, w_in, g_q, g_k, conv_dw_w, conv_dw_b, conv_ln_g, conv_ln_b, w_out, g_ffn, peer_wq, peer_subkeys,
          peer_u, peer_v):
    bsz, t, tp, db, ds = dm.batch, dm.t_prompt, dm.tp, dm.dec_batch, dm.dec_seq
    pr, sr = dm.prompt_rows, dm.sample_rows

    meta = meta_tokens.astype(x_prompt.dtype)
    seq_pad = jnp.zeros((tp - t, D_MODEL), x_prompt.dtype)
    pieces = [p for b in range(bsz) for p in (meta, x_prompt[b], seq_pad)]
    pieces += [x_sample.reshape(sr, D_MODEL), jnp.zeros((dm.rows - pr - sr, D_MODEL), x_prompt.dtype)]
    x_all = jnp.concatenate(pieces, axis=0)

    q_b, k_f, v_f, k_b, v_b, qi_b, kw, ki_b, a_all = _in_projection(x_all, g_attn[0], w_in[0], g_q[0], g_k[0])

    pb, dn, dc = _bias_tiles(rel_bias)
    far = rel_bias[N_BUCKETS - 1]
    cb = jnp.broadcast_to(far[:, None, None], (N_HEADS, 1, QB))
    o_p = _prompt_attention(dm, q_b, qi_b, kw, ki_b, k_b, v_b, pb, cb)

    s0, s1 = pr, pr + sr
    qi_s = qi_b[:, s0:s1].reshape(IDX_HEADS, db, ds, IDX_DIM).transpose(1, 0, 2, 3).reshape(db, IDX_HEADS * ds, IDX_DIM)
    w_col = kw[s0:s1, IDX_DIM:IDX_DIM + IDX_HEADS].reshape(db, ds, IDX_HEADS).transpose(0, 2, 1)
    w_col = w_col.reshape(db, IDX_HEADS * ds, 1)
    q_s = q_b[s0:s1].reshape(db, ds, N_KV_HEADS, GROUP, HEAD_DIM).transpose(0, 2, 3, 1, 4)
    q_s = q_s.reshape(db, N_HEADS * ds, HEAD_DIM)
    pad_keys = lambda z: jnp.pad(z.reshape(db, ds, -1), ((0, 0), (0, PAGE_SIZE - ds), (0, 0)))
    kv_rows = lambda z: pad_keys(z).reshape(db, PAGE_SIZE * N_KV_HEADS, HEAD_DIM)
    by_rows = lambda z: jnp.repeat(z.reshape(N_HEADS * SUBLANES, LANES), N_KV_HEADS, axis=1)
    bfar = jnp.broadcast_to(far[:, None, None], (N_HEADS, SUBLANES, LANES))
    o_s = _decode_attention(
        dm, page_table, cache_k, cache_v, cache_kidx, qi_s, w_col, jnp.swapaxes(pad_keys(ki_b[s0:s1]), 1, 2), q_s,
        kv_rows(k_b[s0:s1]), kv_rows(v_b[s0:s1]), by_rows(bfar), by_rows(dn), by_rows(dc))
    o_s = o_s.reshape(db, N_HEADS, ds, HEAD_DIM).transpose(0, 2, 1, 3).reshape(sr, ATTN_WIDTH)

    cw = _conv_weights(conv_dw_w[0], conv_dw_b[0], conv_ln_g[0], conv_ln_b[0])
    c_p = _prompt_conv(dm, a_all, cw)
    a_s = a_all[s0:s1].reshape(db, ds, CONV_CH)
    ext_s = jnp.concatenate([jnp.zeros((db, CONV_HALO - (CONV_W - 1), CONV_CH), F32), state_conv[0], a_s], axis=1)
    c_s = _sample_conv(dm, ext_s, cw).reshape(sr, CONV_CH)

    tail = jnp.zeros((dm.rows - pr - sr, ATTN_WIDTH), BF16)
    o_all = jnp.concatenate([o_p, o_s.astype(BF16), tail], axis=0)
    c_all = jnp.concatenate([c_p, c_s.astype(BF16), tail], axis=0)
    h, hn2, qp = _out_projection(o_all, c_all, x_all, w_out[0], g_ffn[0], peer_wq[0])

    experts, gates = _peer_route(qp, peer_subkeys[0])
    experts_tok = experts.reshape(PEER_SLOTS, dm.rows).T
    gates_tok = gates.reshape(PEER_SLOTS, dm.rows).T
    uv = jnp.concatenate([_pack_bf16_pairs(peer_u[0]), _pack_bf16_pairs(peer_v[0])], axis=1)[:, None, :]
    y = _peer_ffn(experts_tok, gates_tok, hn2, h, uv)

    per_seq = lambda z, lo, hi, w: jnp.stack([z[b * tp + lo:b * tp + hi, :w] for b in range(bsz)])
    y_prompt = per_seq(y, N_META, t, D_MODEL)
    y_sample = y[s0:s1].reshape(db, ds, D_MODEL)
    kv_shape = (N_KV_HEADS, HEAD_DIM)
    new_k_p = per_seq(k_f, 0, t, KV_WIDTH).reshape(1, bsz, t, *kv_shape)
    new_v_p = per_seq(v_f, 0, t, KV_WIDTH).reshape(1, bsz, t, *kv_shape)
    new_ki_p = per_seq(kw, 0, t, IDX_DIM)[None]
    new_conv_p = per_seq(a_all, t - (CONV_W - 1), t, CONV_CH)[None]
    new_k_s = k_f[s0:s1].reshape(1, db, ds, *kv_shape)
    new_v_s = v_f[s0:s1].reshape(1, db, ds, *kv_shape)
    new_ki_s = kw[s0:s1, :IDX_DIM].reshape(1, db, ds, IDX_DIM)
    new_conv_s = ext_s[:, -(CONV_W - 1):][None]
    return (y_prompt, y_sample, new_k_p, new_v_p, new_ki_p, new_conv_p, new_k_s, new_v_s, new_ki_s, new_conv_s)


def kernel(x_prompt, x_sample, cache_k, cache_v, cache_kidx, state_conv, page_table, meta_tokens, rel_bias, g_attn,
           w_in, g_q, g_k, conv_dw_w, conv_dw_b, conv_ln_g, conv_ln_b, w_out, g_ffn, peer_wq, peer_subkeys, peer_u,
           peer_v):
    assert g_attn.shape[0] == 1, "single trunk layer"
    dm = Dims(batch=x_prompt.shape[0], seq=x_prompt.shape[1], dec_batch=x_sample.shape[0],
              dec_seq=x_sample.shape[1], past_len=page_table.shape[1] * PAGE_SIZE)
    assert dm.dec_seq == SUBLANES and dm.t_prompt >= CONV_W - 1
    return _step(dm, x_prompt, x_sample, cache_k, cache_v, cache_kidx, state_conv, page_table, meta_tokens, rel_bias,
                 g_attn, w_in, g_q, g_k, conv_dw_w, conv_dw_b, conv_ln_g, conv_ln_b, w_out, g_ffn, peer_wq,
                 peer_subkeys, peer_u, peer_v)
```

```python
import functools
import math
from typing import NamedTuple

import jax
import jax.numpy as jnp
from jax import lax
from jax.experimental import pallas as pl
from jax.experimental.pallas import tpu as pltpu

F32 = jnp.float32
BF16 = jnp.bfloat16
I32 = jnp.int32

D_MODEL = 2048
N_META = 16
N_HEADS = 8
HEAD_DIM = 128
N_KV_HEADS = 2
GROUP = N_HEADS // N_KV_HEADS
ATTN_WIDTH = N_HEADS * HEAD_DIM
KV_WIDTH = N_KV_HEADS * HEAD_DIM
CONV_CH = D_MODEL - ATTN_WIDTH
CONV_W = 31
IDX_HEADS = 16
IDX_DIM = 64
IDX_SCALE = (IDX_HEADS * IDX_DIM) ** -0.5
TOPK_MAX = 256
N_BUCKETS = 32
MAX_DISTANCE = 128
ATTN_SCALE = HEAD_DIM ** -0.5
PAGE_SIZE = 128
PEER_HEADS = 8
PEER_NKEYS = 128
PEER_KDIM = 128
PEER_TOPK = 16
PEER_SLOTS = PEER_HEADS * PEER_TOPK
EPS = 1e-6

LANES = 128
SUBLANES = 8
QB = 128
ROW_TILE = 256
CONV_HALO = 32
PEER_TOK_BLOCK = 256
PEER_UNROLL = 8
PEER_BUFS = 8
PEER_AHEAD = PEER_BUFS - 1
PEER_DMA_QUEUES = 2
INT_MIN = -(2 ** 31)
NEG_BIG = -1e30
M_INIT = -1e29
VMEM_LIMIT = 56 * 1024 * 1024

NT_DIMS = (((1,), (1,)), ((), ()))
TN_DIMS = (((0,), (0,)), ((), ()))


class Dims(NamedTuple):
    batch: int
    seq: int
    dec_batch: int
    dec_seq: int
    past_len: int

    @property
    def t_prompt(self):
        return self.seq + N_META

    @property
    def nq(self):
        return -(-self.t_prompt // QB)

    @property
    def tp(self):
        return self.nq * QB

    @property
    def prompt_rows(self):
        return self.batch * self.tp

    @property
    def sample_rows(self):
        return self.dec_batch * self.dec_seq

    @property
    def rows(self):
        r = self.prompt_rows + self.sample_rows
        return -(-r // ROW_TILE) * ROW_TILE

    @property
    def n_pages(self):
        return self.past_len // PAGE_SIZE


def _dot(a, b):
    return jnp.dot(a, b, preferred_element_type=F32)


def _dot_nt(a, b):
    return lax.dot_general(a, b, NT_DIMS, preferred_element_type=F32)


def _dot_tn(a, b):
    return lax.dot_general(a, b, TN_DIMS, preferred_element_type=F32)


def _params(sem):
    return pltpu.CompilerParams(dimension_semantics=sem, vmem_limit_bytes=VMEM_LIMIT)


def _const_spec(shape):
    nd = len(shape)
    return pl.BlockSpec(shape, lambda *_: (0,) * nd)


def _weight_spec(shape):
    nd = len(shape)
    return pl.BlockSpec(shape, lambda *_: (0,) * nd, pipeline_mode=pl.Buffered(1))


def _order_key(x):
    bits = pltpu.bitcast(x + 0.0, I32)
    return bits ^ ((bits >> 31) & 0x7FFFFFFF)


def _inproj_kernel(x_ref, ga_ref, wq_ref, wkv_ref, wqi_ref, wkw_ref, wu_ref, gq_ref, gk_ref,
                   q_ref, kf_ref, vf_ref, kb_ref, vb_ref, qi_ref, kw_ref, kib_ref, a_ref):
    x = x_ref[...]
    ms = jnp.mean(x * x, axis=-1, keepdims=True)
    hn = (x * lax.rsqrt(ms + EPS) * ga_ref[...]).astype(BF16)

    def head_norm(z, g):
        zm = jnp.mean(z * z, axis=-1, keepdims=True)
        return z * lax.rsqrt(zm + EPS) * g

    q = _dot(hn, wq_ref[...])
    for h in range(N_HEADS):
        sl = slice(h * HEAD_DIM, (h + 1) * HEAD_DIM)
        q_ref[:, sl] = (head_norm(q[:, sl], gq_ref[...]) * ATTN_SCALE).astype(BF16)
    kv = _dot(hn, wkv_ref[...])
    for n in range(N_KV_HEADS):
        sl = slice(n * HEAD_DIM, (n + 1) * HEAD_DIM)
        kn = head_norm(kv[:, sl], gk_ref[...])
        kf_ref[:, sl] = kn
        kb_ref[:, sl] = kn.astype(BF16)
    v = kv[:, KV_WIDTH:]
    vf_ref[...] = v
    vb_ref[...] = v.astype(BF16)
    qi = _dot(hn, wqi_ref[...]).astype(BF16)
    for h in range(IDX_HEADS):
        qi_ref[h] = qi[:, h * IDX_DIM:(h + 1) * IDX_DIM]
    kw = _dot(hn, wkw_ref[...])
    lane = lax.broadcasted_iota(I32, kw.shape, 1)
    kw = jnp.where(lane >= IDX_DIM, kw * IDX_SCALE, kw)
    kw_ref[...] = kw
    kib_ref[...] = kw[:, :IDX_DIM].astype(BF16)
    u = _dot(hn, wu_ref[...])
    a_ref[...] = u[:, :CONV_CH] * jax.nn.sigmoid(u[:, CONV_CH:])


def _in_projection(x_all, g_attn, w_in, g_q, g_k):
    rows = x_all.shape[0]
    o_q = ATTN_WIDTH
    o_k = o_q + KV_WIDTH
    o_v = o_k + KV_WIDTH
    o_qi = o_v + IDX_HEADS * IDX_DIM
    o_ki = o_qi + IDX_DIM
    o_wi = o_ki + IDX_HEADS
    wq = w_in[:, :o_q].astype(BF16)
    wkv = w_in[:, o_q:o_v].astype(BF16)
    wqi = w_in[:, o_v:o_qi].astype(BF16)
    wkw = jnp.pad(w_in[:, o_qi:o_wi], ((0, 0), (0, LANES - IDX_DIM - IDX_HEADS))).astype(BF16)
    wu = w_in[:, o_wi:].astype(BF16)
    row = lambda w: pl.BlockSpec((ROW_TILE, w), lambda i: (i, 0))
    outs = [
        (ATTN_WIDTH, BF16), (KV_WIDTH, F32), (KV_WIDTH, F32), (KV_WIDTH, BF16), (KV_WIDTH, BF16),
        None, (LANES, F32), (IDX_DIM, BF16), (CONV_CH, F32),
    ]
    qi_spec = pl.BlockSpec((IDX_HEADS, ROW_TILE, IDX_DIM), lambda i: (0, i, 0))
    qi_shape = jax.ShapeDtypeStruct((IDX_HEADS, rows, IDX_DIM), BF16)
    return pl.pallas_call(
        _inproj_kernel,
        grid=(rows // ROW_TILE,),
        in_specs=[row(D_MODEL), _const_spec((1, D_MODEL)), _weight_spec(wq.shape), _weight_spec(wkv.shape),
                  _weight_spec(wqi.shape), _weight_spec(wkw.shape), _weight_spec(wu.shape),
                  _const_spec((1, HEAD_DIM)), _const_spec((1, HEAD_DIM))],
        out_specs=[qi_spec if o is None else row(o[0]) for o in outs],
        out_shape=[qi_shape if o is None else jax.ShapeDtypeStruct((rows, o[0]), o[1]) for o in outs],
        compiler_params=_params(("arbitrary",)),
        name="in_projection",
    )(x_all, g_attn.reshape(1, D_MODEL), wq, wkv, wqi, wkw, wu, g_q.reshape(1, HEAD_DIM), g_k.reshape(1, HEAD_DIM))


def _t5_bias_of(dist, rb_ref, h):
    n = jnp.maximum(dist, 0)
    max_exact = N_BUCKETS // 2
    nf = jnp.maximum(n, 1).astype(F32)
    large = max_exact + (jnp.log(nf / max_exact) / math.log(MAX_DISTANCE / max_exact)
                         * (N_BUCKETS - max_exact)).astype(I32)
    large = jnp.minimum(large, N_BUCKETS - 1)
    bucket = jnp.where(n < max_exact, n, large)
    out = jnp.zeros(dist.shape, F32)
    for k in range(N_BUCKETS):
        out = jnp.where(bucket == k, rb_ref[k, h], out)
    return out


def _bias_kernel(rb_ref, pb_ref, dn_ref, dc_ref):
    r2 = lax.broadcasted_iota(I32, (2 * QB, QB), 0)
    c2 = lax.broadcasted_iota(I32, (2 * QB, QB), 1)
    r1 = lax.broadcasted_iota(I32, (SUBLANES, LANES), 0)
    c1 = lax.broadcasted_iota(I32, (SUBLANES, LANES), 1)
    for h in range(N_HEADS):
        pb_ref[h] = _t5_bias_of(QB + c2 - r2, rb_ref, h)
        dn_ref[h] = _t5_bias_of(PAGE_SIZE + r1 - c1, rb_ref, h)
        dc_ref[h] = _t5_bias_of(r1 - c1, rb_ref, h)


def _bias_tiles(rel_bias):
    return pl.pallas_call(
        _bias_kernel,
        in_specs=[pl.BlockSpec(memory_space=pltpu.SMEM)],
        out_shape=[jax.ShapeDtypeStruct((N_HEADS, 2 * QB, QB), F32),
                   jax.ShapeDtypeStruct((N_HEADS, SUBLANES, LANES), F32),
                   jax.ShapeDtypeStruct((N_HEADS, SUBLANES, LANES), F32)],
        name="t5_bias_tiles",
    )(rel_bias)


def _prompt_attn_kernel(qi_ref, q_ref, kw_ref, ki_ref, k_ref, v_ref, pb_ref, cb_ref, o_ref,
                        key_sc, acc_sc, m_sc, l_sc, *, topk, lc):
    i = pl.program_id(1)
    q_lo = i * QB
    n_chunks = (q_lo + QB + lc - 1) // lc
    w_t = kw_ref[...].T
    row_l = lax.broadcasted_iota(I32, (lc, QB), 0)
    qpos_l = lax.broadcasted_iota(I32, (lc, QB), 1) + q_lo

    def score_chunk(c, carry):
        base = pl.multiple_of(c * lc, lc)
        kc = ki_ref[pl.ds(base, lc), :]
        acc = jnp.zeros((lc, QB), F32)
        for hp in range(IDX_HEADS // 2):
            s2 = _dot_nt(kc, qi_ref[2 * hp:2 * hp + 2].reshape(2 * QB, IDX_DIM))
            for h in (2 * hp, 2 * hp + 1):
                s = s2[:, (h % 2) * QB:(h % 2 + 1) * QB]
                acc = acc + w_t[IDX_DIM + h:IDX_DIM + h + 1, :] * jnp.maximum(s, 0.0)
        key = jnp.where(row_l + base <= qpos_l, _order_key(acc), INT_MIN)
        key_sc[pl.ds(base, lc), :] = key
        return carry

    lax.fori_loop(0, n_chunks, score_chunk, 0)

    def count_keys(pred):
        def count_chunk(c, cnt):
            base = pl.multiple_of(c * lc, lc)
            hit = pred(key_sc[pl.ds(base, lc), :], row_l + base).astype(I32)
            return cnt + jnp.sum(hit.reshape(lc // SUBLANES, SUBLANES, QB), axis=0)

        cnt = lax.fori_loop(0, n_chunks, count_chunk, jnp.zeros((SUBLANES, QB), I32))
        return jnp.sum(cnt, axis=0, keepdims=True)

    def bisect(bit, thr):
        cand = thr + lax.shift_left(jnp.int32(1), 31 - bit)
        return jnp.where(count_keys(lambda k, l: k >= cand) >= topk, cand, thr)

    thr = lax.fori_loop(0, 32, bisect, jnp.full((1, QB), INT_MIN, I32))

    surplus = count_keys(lambda k, l: k >= thr) - topk
    tied = (surplus > 0) & (thr > INT_MIN)

    @pl.when(jnp.max(tied.astype(I32)) > 0)
    def _():
        need = topk - count_keys(lambda k, l: k > thr)
        pos_bits = key_sc.shape[0].bit_length()

        def bisect_pos(bit, cut):
            cand = cut + lax.shift_left(jnp.int32(1), pos_bits - 1 - bit)
            return jnp.where(count_keys(lambda k, l: (k == thr) & (l < cand)) < need, cand, cut)

        cut = lax.fori_loop(0, pos_bits, bisect_pos, jnp.zeros((1, QB), I32))

        def drop_chunk(c, carry):
            base = pl.multiple_of(c * lc, lc)
            k = key_sc[pl.ds(base, lc), :]
            drop = tied & (k == thr) & (row_l + base > cut)
            key_sc[pl.ds(base, lc), :] = jnp.where(drop, INT_MIN, k)
            return carry

        lax.fori_loop(0, n_chunks, drop_chunk, 0)

    thr = jnp.maximum(thr, INT_MIN + 1)

    m_sc[...] = jnp.full(m_sc.shape, M_INIT, F32)
    l_sc[...] = jnp.zeros(l_sc.shape, F32)
    acc_sc[...] = jnp.zeros(acc_sc.shape, F32)

    def attend(base, rows, sel, bias_of, bias_is_const):
        kc = k_ref[pl.ds(base, rows), :]
        vc = v_ref[pl.ds(base, rows), :]
        for n in range(N_KV_HEADS):
            kn = kc[:, n * HEAD_DIM:(n + 1) * HEAD_DIM]
            vn = vc[:, n * HEAD_DIM:(n + 1) * HEAD_DIM]
            for g in range(GROUP):
                h = n * GROUP + g
                lg = _dot_nt(kn, q_ref[:, h * HEAD_DIM:(h + 1) * HEAD_DIM])
                m_old = m_sc[h:h + 1, :]
                if bias_is_const:
                    lg = jnp.where(sel, lg, NEG_BIG)
                    m_new = jnp.maximum(m_old, jnp.max(lg, axis=0, keepdims=True) + bias_of(h))
                    p = jnp.exp(lg - (m_new - bias_of(h)))
                else:
                    lg = jnp.where(sel, lg + bias_of(h), NEG_BIG)
                    m_new = jnp.maximum(m_old, jnp.max(lg, axis=0, keepdims=True))
                    p = jnp.exp(lg - m_new)
                alpha = jnp.exp(m_old - m_new)
                l_sc[h:h + 1, :] = alpha * l_sc[h:h + 1, :] + jnp.sum(p, axis=0, keepdims=True)
                acc_sc[h] = alpha * acc_sc[h] + _dot_tn(vn, p.astype(BF16))
                m_sc[h:h + 1, :] = m_new

    far_hi = jnp.maximum(q_lo - QB, 0)
    n_far = (far_hi + lc - 1) // lc

    def far_chunk(c, carry):
        base = pl.multiple_of(c * lc, lc)
        sel = (key_sc[pl.ds(base, lc), :] >= thr) & (row_l + base < far_hi)
        attend(base, lc, sel, lambda h: cb_ref[h], True)
        return carry

    lax.fori_loop(0, n_far, far_chunk, 0)

    prev_lo = pl.multiple_of(jnp.maximum(q_lo - QB, 0), QB)
    sel_prev = (key_sc[pl.ds(prev_lo, QB), :] >= thr) & (i > 0)
    attend(prev_lo, QB, sel_prev, lambda h: pb_ref[h, :QB, :], False)
    diag_lo = pl.multiple_of(q_lo, QB)
    sel_diag = key_sc[pl.ds(diag_lo, QB), :] >= thr
    attend(diag_lo, QB, sel_diag, lambda h: pb_ref[h, QB:, :], False)

    for h in range(N_HEADS):
        o_t = acc_sc[h] / l_sc[h:h + 1, :]
        o_ref[:, h * HEAD_DIM:(h + 1) * HEAD_DIM] = o_t.T.astype(BF16)


def _prompt_attention(dm, q_b, qi_b, kw, ki_b, k_b, v_b, pb, cb):
    nq, tp = dm.nq, dm.tp
    topk = min(TOPK_MAX, dm.seq // 4)
    m = max(d for d in (1, 2, 3, 4) if nq % d == 0)
    lc = m * QB
    qblk = lambda w: pl.BlockSpec((QB, w), lambda b, i: (b * nq + i, 0))
    kblk = lambda w: pl.BlockSpec((tp, w), lambda b, i: (b, 0))
    qi_blk = pl.BlockSpec((IDX_HEADS, QB, IDX_DIM), lambda b, i: (0, b * nq + i, 0))
    return pl.pallas_call(
        functools.partial(_prompt_attn_kernel, topk=topk, lc=lc),
        grid=(dm.batch, nq),
        in_specs=[qi_blk, qblk(ATTN_WIDTH), qblk(LANES),
                  kblk(IDX_DIM), kblk(KV_WIDTH), kblk(KV_WIDTH),
                  _const_spec(pb.shape), _const_spec(cb.shape)],
        out_specs=qblk(ATTN_WIDTH),
        out_shape=jax.ShapeDtypeStruct((dm.prompt_rows, ATTN_WIDTH), BF16),
        scratch_shapes=[pltpu.VMEM((tp, QB), I32), pltpu.VMEM((N_HEADS, HEAD_DIM, QB), F32),
                        pltpu.VMEM((N_HEADS, QB), F32), pltpu.VMEM((N_HEADS, QB), F32)],
        compiler_params=_params(("arbitrary", "arbitrary")),
        name="prompt_attention",
    )(qi_b, q_b, kw, ki_b, k_b, v_b, pb, cb)


DEC_PAGES_MAX = 32


def _decode_kernel(pt_ref, qi_ref, w_ref, q_ref, kicur_ref, kcur_ref, vcur_ref, bfar_ref, bnear_ref, bcur_ref, dup_ref,
                   *refs, topk, n_pages, dec_seq, pps):
    ki_refs = refs[:pps]
    k_refs = refs[pps:2 * pps]
    v_refs = refs[2 * pps:3 * pps]
    o_ref, key_sc, thr_sc, m_sc, l_sc, acc_sc = refs[3 * pps:]
    n_groups = n_pages // pps
    s = pl.program_id(1)
    n_rows = N_KV_HEADS * GROUP * SUBLANES
    n_cols = N_KV_HEADS * PAGE_SIZE

    def scores(keys_t_bf):
        sc = _dot(qi_ref[0], keys_t_bf)
        acc = jnp.zeros((SUBLANES, LANES), F32)
        for h in range(IDX_HEADS):
            rows = slice(h * SUBLANES, (h + 1) * SUBLANES)
            acc = acc + w_ref[0, rows, :] * jnp.maximum(sc[rows, :], 0.0)
        return _order_key(acc)

    @pl.when(s < n_groups)
    def _():
        for j in range(pps):
            key_sc[s * pps + j] = scores(ki_refs[j][...].astype(BF16))

    @pl.when(s == n_groups - 1)
    def _():
        r = lax.broadcasted_iota(I32, (SUBLANES, LANES), 0)
        c = lax.broadcasted_iota(I32, (SUBLANES, LANES), 1)
        key_sc[n_pages] = jnp.where((c <= r) & (c < dec_seq), scores(kicur_ref[0]), INT_MIN)

        page_of = lax.broadcasted_iota(I32, (pps, SUBLANES, LANES), 0)
        lane_of = lax.broadcasted_iota(I32, (pps, SUBLANES, LANES), 2)

        def count_keys(pred):
            cnt = pred(key_sc[n_pages], c + n_pages * PAGE_SIZE).astype(I32)
            for blk in range(n_groups):
                pos = (page_of + blk * pps) * PAGE_SIZE + lane_of
                cnt = cnt + jnp.sum(pred(key_sc[pl.ds(blk * pps, pps)], pos).astype(I32), axis=0)
            return jnp.sum(cnt, axis=1, keepdims=True)

        def bisect(bit, thr):
            cand = thr + lax.shift_left(jnp.int32(1), 31 - bit)
            return jnp.where(count_keys(lambda k, l: k >= cand) >= topk, cand, thr)

        thr = lax.fori_loop(0, 32, bisect, jnp.full((SUBLANES, 1), INT_MIN, I32))

        surplus = count_keys(lambda k, l: k >= thr) - topk
        tied = (surplus > 0) & (thr > INT_MIN)

        @pl.when(jnp.max(tied.astype(I32)) > 0)
        def _():
            need = topk - count_keys(lambda k, l: k > thr)
            pos_bits = ((n_pages + 1) * PAGE_SIZE).bit_length()

            def bisect_pos(bit, cut):
                cand = cut + lax.shift_left(jnp.int32(1), pos_bits - 1 - bit)
                return jnp.where(count_keys(lambda k, l: (k == thr) & (l < cand)) < need, cand, cut)

            cut = lax.fori_loop(0, pos_bits, bisect_pos, jnp.zeros((SUBLANES, 1), I32))

            def drop_page(j, carry):
                k = key_sc[j]
                drop = tied & (k == thr) & (c + j * PAGE_SIZE > cut)
                key_sc[j] = jnp.where(drop, INT_MIN, k)
                return carry

            lax.fori_loop(0, n_pages + 1, drop_page, 0)

        thr_sc[...] = jnp.broadcast_to(jnp.maximum(thr, INT_MIN + 1), (SUBLANES, LANES))
        m_sc[...] = jnp.full(m_sc.shape, M_INIT, F32)
        l_sc[...] = jnp.zeros(l_sc.shape, F32)
        acc_sc[...] = jnp.zeros(acc_sc.shape, F32)

    def attend(key_tiles, k_of, v_of, bias_of):
        row_head = lax.broadcasted_iota(I32, (n_rows, n_cols), 0) // (GROUP * SUBLANES)
        col_head = lax.broadcasted_iota(I32, (n_rows, n_cols), 1) % N_KV_HEADS
        own_head = row_head == col_head
        thr = thr_sc[...]
        lgs = []
        for j, kt in enumerate(key_tiles):
            picked = jnp.where(kt >= thr, 1.0, 0.0).astype(BF16)
            picked = _dot(picked, dup_ref[...])
            picked = jnp.concatenate([picked] * (n_rows // SUBLANES), axis=0)
            lg = _dot_nt(q_ref[0], k_of(j)) + bias_of(j)
            lgs.append(jnp.where(jnp.where(own_head, picked, 0.0) > 0.5, lg, NEG_BIG))
        m_old = m_sc[...]
        m_new = jnp.maximum(m_old, jnp.max(functools.reduce(jnp.maximum, lgs), axis=1, keepdims=True))
        prs = [jnp.exp(lg - m_new) for lg in lgs]
        alpha = jnp.exp(m_old - m_new)
        l_sc[...] = alpha * l_sc[...] + jnp.sum(functools.reduce(jnp.add, prs), axis=1, keepdims=True)
        pv = functools.reduce(jnp.add, [_dot(pr.astype(BF16), v_of(j)) for j, pr in enumerate(prs)])
        acc_sc[...] = alpha * acc_sc[...] + pv
        m_sc[...] = m_new

    @pl.when(s >= n_groups)
    def _():
        g = s - n_groups
        last_group = g == n_groups - 1

        def bias_of(j):
            if j == pps - 1:
                return jnp.where(last_group, bnear_ref[...], bfar_ref[...])
            return bfar_ref[...]

        attend([key_sc[g * pps + j] for j in range(pps)],
               lambda j: k_refs[j][...].astype(BF16), lambda j: v_refs[j][...].astype(BF16), bias_of)

    @pl.when(s == 2 * n_groups - 1)
    def _():
        attend([key_sc[n_pages]], lambda j: kcur_ref[0], lambda j: vcur_ref[0], lambda j: bcur_ref[...])
        o_ref[0] = acc_sc[...] / l_sc[...]


def _decode_attention(dm, page_table, cache_k, cache_v, cache_kidx, qi_s, w_col, ki_cur, q_s, k_cur, v_cur,
                      bfar, bnear, bcur):
    db, np_ = dm.dec_batch, dm.n_pages
    topk = min(TOPK_MAX, (dm.past_len + dm.dec_seq) // 4)
    rows_hq = IDX_HEADS * SUBLANES
    n_rows = N_KV_HEADS * GROUP * SUBLANES
    n_cols = N_KV_HEADS * PAGE_SIZE
    pps = max(d for d in range(1, DEC_PAGES_MAX + 1) if np_ % d == 0)
    ng = np_ // pps
    ki_t = jnp.swapaxes(cache_kidx, 2, 3)
    k_rows = cache_k.reshape(cache_k.shape[:2] + (n_cols, HEAD_DIM))
    v_rows = cache_v.reshape(cache_v.shape[:2] + (n_cols, HEAD_DIM))
    dup = (jnp.arange(n_cols)[None, :] // N_KV_HEADS == jnp.arange(PAGE_SIZE)[:, None]).astype(BF16)
    per_b = lambda *shape: pl.BlockSpec((1,) + shape, lambda b, s, pt: (b,) + (0,) * len(shape))
    const = lambda shape: pl.BlockSpec(shape, lambda b, s, pt: (0,) * len(shape))
    ki_page = lambda j: pl.BlockSpec(
        (None, None, IDX_DIM, PAGE_SIZE), lambda b, s, pt: (0, pt[b, jnp.minimum(s, ng - 1) * pps + j], 0, 0))
    kv_page = lambda j: pl.BlockSpec(
        (None, None, n_cols, HEAD_DIM), lambda b, s, pt: (0, pt[b, jnp.maximum(s - ng, 0) * pps + j], 0, 0))
    pages = range(pps)
    return pl.pallas_call(
        functools.partial(_decode_kernel, topk=topk, n_pages=np_, dec_seq=dm.dec_seq, pps=pps),
        grid_spec=pltpu.PrefetchScalarGridSpec(
            num_scalar_prefetch=1, grid=(db, 2 * ng),
            in_specs=[per_b(rows_hq, IDX_DIM), per_b(rows_hq, 1), per_b(n_rows, HEAD_DIM),
                      per_b(IDX_DIM, PAGE_SIZE), per_b(n_cols, HEAD_DIM), per_b(n_cols, HEAD_DIM),
                      const(bfar.shape), const(bnear.shape), const(bcur.shape), const(dup.shape)]
                     + [ki_page(j) for j in pages] + [kv_page(j) for j in pages] + [kv_page(j) for j in pages],
            out_specs=per_b(n_rows, HEAD_DIM),
            scratch_shapes=[pltpu.VMEM((np_ + 1, SUBLANES, LANES), I32), pltpu.VMEM((SUBLANES, LANES), I32),
                            pltpu.VMEM((n_rows, 1), F32), pltpu.VMEM((n_rows, 1), F32),
                            pltpu.VMEM((n_rows, HEAD_DIM), F32)]),
        out_shape=jax.ShapeDtypeStruct((db, n_rows, HEAD_DIM), F32),
        compiler_params=_params(("arbitrary", "arbitrary")),
        name="decode_attention",
    )(page_table, qi_s, w_col, q_s, ki_cur, k_cur, v_cur, bfar, bnear, bcur, dup,
      *([ki_t] * pps), *([k_rows] * pps), *([v_rows] * pps))


CONV_CT = 256


def _conv_ln_swish(ext_ref, rows, w_ref, b_ref, g_ref, beta_ref, y_sc, out_ref_setter):
    off = CONV_HALO - (CONV_W - 1)
    for ct in range(CONV_CH // CONV_CT):
        cs = slice(ct * CONV_CT, (ct + 1) * CONV_CT)
        acc = jnp.zeros((rows, CONV_CT), F32)
        for j in range(CONV_W):
            acc = acc + ext_ref[pl.ds(off + j, rows), cs] * w_ref[j:j + 1, cs]
        y_sc[:, cs] = acc + b_ref[:, cs]
    y = y_sc[...]
    mu = jnp.mean(y, axis=-1, keepdims=True)
    d = y - mu
    var = jnp.mean(d * d, axis=-1, keepdims=True)
    yn = d * lax.rsqrt(var + EPS) * g_ref[...] + beta_ref[...]
    out_ref_setter(yn * jax.nn.sigmoid(yn))


def _prompt_conv_kernel(prev_ref, cur_ref, w_ref, b_ref, g_ref, beta_ref, c_ref, ext_sc, y_sc):
    i = pl.program_id(1)
    halo = prev_ref[QB - CONV_HALO:, :]
    ext_sc[:CONV_HALO, :] = jnp.where(i > 0, halo, 0.0)
    ext_sc[CONV_HALO:, :] = cur_ref[...]

    def put(c):
        c_ref[...] = c.astype(BF16)

    _conv_ln_swish(ext_sc, QB, w_ref, b_ref, g_ref, beta_ref, y_sc, put)


def _sample_conv_kernel(ext_ref, w_ref, b_ref, g_ref, beta_ref, c_ref, y_sc, *, rows):
    def put(c):
        c_ref[0] = c

    _conv_ln_swish(ext_ref.at[0], rows, w_ref, b_ref, g_ref, beta_ref, y_sc, put)


def _conv_weights(dw_w, dw_b, ln_g, ln_b):
    return (dw_w, dw_b.reshape(1, CONV_CH), ln_g.reshape(1, CONV_CH), ln_b.reshape(1, CONV_CH))


def _prompt_conv(dm, a_all, cw):
    nq = dm.nq
    blk = lambda f: pl.BlockSpec((QB, CONV_CH), f)
    return pl.pallas_call(
        _prompt_conv_kernel,
        grid=(dm.batch, nq),
        in_specs=[blk(lambda b, i: (jnp.maximum(b * nq + i - 1, 0), 0)), blk(lambda b, i: (b * nq + i, 0)),
                  _const_spec((CONV_W, CONV_CH))] + [_const_spec((1, CONV_CH))] * 3,
        out_specs=blk(lambda b, i: (b * nq + i, 0)),
        out_shape=jax.ShapeDtypeStruct((dm.prompt_rows, CONV_CH), BF16),
        scratch_shapes=[pltpu.VMEM((CONV_HALO + QB, CONV_CH), F32), pltpu.VMEM((QB, CONV_CH), F32)],
        compiler_params=_params(("arbitrary", "arbitrary")),
        name="prompt_conv",
    )(a_all, a_all, *cw)


def _sample_conv(dm, ext_s, cw):
    rows = dm.dec_seq
    tot = CONV_HALO + rows
    return pl.pallas_call(
        functools.partial(_sample_conv_kernel, rows=rows),
        grid=(dm.dec_batch,),
        in_specs=[pl.BlockSpec((1, tot, CONV_CH), lambda b: (b, 0, 0)), _const_spec((CONV_W, CONV_CH))]
                 + [_const_spec((1, CONV_CH))] * 3,
        out_specs=pl.BlockSpec((1, rows, CONV_CH), lambda b: (b, 0, 0)),
        out_shape=jax.ShapeDtypeStruct((dm.dec_batch, rows, CONV_CH), F32),
        scratch_shapes=[pltpu.VMEM((rows, CONV_CH), F32)],
        compiler_params=_params(("arbitrary",)),
        name="sample_conv",
    )(ext_s, *cw)


def _outproj_kernel(o_ref, c_ref, x_ref, wo_ref, wc_ref, g_ref, wpq_ref, h_ref, hn_ref, qp_ref):
    h = x_ref[...] + _dot(o_ref[...], wo_ref[...]) + _dot(c_ref[...], wc_ref[...])
    h_ref[...] = h
    ms = jnp.mean(h * h, axis=-1, keepdims=True)
    hn = h * lax.rsqrt(ms + EPS) * g_ref[...]
    hn_ref[...] = hn
    qp_ref[...] = _dot(hn.astype(BF16), wpq_ref[...]).astype(BF16)


def _out_projection(o_all, c_all, x_all, w_out, g_ffn, peer_wq):
    rows = x_all.shape[0]
    wo = w_out[:ATTN_WIDTH].astype(BF16)
    wc = w_out[ATTN_WIDTH:].astype(BF16)
    wpq = peer_wq.astype(BF16)
    row = lambda w: pl.BlockSpec((ROW_TILE, w), lambda i: (i, 0))
    pq = PEER_HEADS * PEER_KDIM
    return pl.pallas_call(
        _outproj_kernel,
        grid=(rows // ROW_TILE,),
        in_specs=[row(ATTN_WIDTH), row(CONV_CH), row(D_MODEL), _weight_spec(wo.shape), _weight_spec(wc.shape),
                  _const_spec((1, D_MODEL)), _weight_spec(wpq.shape)],
        out_specs=[row(D_MODEL), row(D_MODEL), row(pq)],
        out_shape=[jax.ShapeDtypeStruct((rows, D_MODEL), F32), jax.ShapeDtypeStruct((rows, D_MODEL), F32),
                   jax.ShapeDtypeStruct((rows, pq), BF16)],
        compiler_params=_params(("arbitrary",)),
        name="out_projection",
    )(o_all, c_all, x_all, wo, wc, g_ffn.reshape(1, D_MODEL), wpq)


def _extract_top(vals, pos, payload, count):
    big = jnp.int32(2 ** 30)
    out_v, out_p = [], []
    for _ in range(count):
        m = jnp.max(vals, axis=0, keepdims=True)
        first = jnp.min(jnp.where(vals == m, pos, big), axis=0, keepdims=True)
        hit = pos == first
        out_v.append(m)
        out_p.append(first if payload is None else jnp.sum(jnp.where(hit, payload, 0), axis=0, keepdims=True))
        vals = jnp.where(hit, -jnp.inf, vals)
    return jnp.concatenate(out_v, axis=0), jnp.concatenate(out_p, axis=0)


def _product_candidates(v1, i1, v2, i2):
    k = PEER_TOPK
    r8 = lax.broadcasted_iota(I32, (SUBLANES, LANES), 0)
    r16 = lax.broadcasted_iota(I32, (k, LANES), 0)
    vals = [v1[0:1] + v2]
    poss = [r16]
    idxs = [i1[0:1] * PEER_NKEYS + i2]
    for a in (1, 2, 3):
        vals.append(v1[a:a + 1] + v2[:SUBLANES])
        poss.append(a * k + r8)
        idxs.append(i1[a:a + 1] * PEER_NKEYS + i2[:SUBLANES])
    for b in (0, 1, 2):
        vals.append(jnp.where(r8 >= 4, v1[:SUBLANES] + v2[b:b + 1], -jnp.inf))
        poss.append(jnp.where(r8 >= 4, r8 * k + b, -1))
        idxs.append(i1[:SUBLANES] * PEER_NKEYS + i2[b:b + 1])
    vals.append(v1[SUBLANES:] + v2[0:1])
    poss.append((r8 + SUBLANES) * k)
    idxs.append(i1[SUBLANES:] * PEER_NKEYS + i2[0:1])
    cat = lambda xs: jnp.concatenate(xs, axis=0)
    return cat(vals), cat(poss), cat(idxs)


def _route_kernel(qp_ref, sk_ref, e_ref, g_ref):
    half = PEER_KDIM // 2
    key_id = lax.broadcasted_iota(I32, (PEER_NKEYS, LANES), 0)
    for h in range(PEER_HEADS):
        tops = []
        for c in range(2):
            qs = qp_ref[:, h * PEER_KDIM + c * half:h * PEER_KDIM + (c + 1) * half]
            s = _dot_nt(sk_ref[c], qs)
            tops.append(_extract_top(s, key_id, None, PEER_TOPK))
        (v1, i1), (v2, i2) = tops
        cand, cpos, cidx = _product_candidates(v1, i1, v2, i2)
        vals, experts = _extract_top(cand, cpos, cidx, PEER_TOPK)
        ex = jnp.exp(vals - vals[0:1, :])
        g_ref[h] = ex / jnp.sum(ex, axis=0, keepdims=True)
        e_ref[h] = experts


def _peer_route(qp, sub_keys):
    rows = qp.shape[0]
    sk = sub_keys.astype(BF16)
    out = pl.BlockSpec((PEER_HEADS, PEER_TOPK, LANES), lambda i: (0, 0, i))
    return pl.pallas_call(
        _route_kernel,
        grid=(rows // LANES,),
        in_specs=[pl.BlockSpec((LANES, PEER_HEADS * PEER_KDIM), lambda i: (i, 0)), _const_spec(sk.shape)],
        out_specs=[out, out],
        out_shape=[jax.ShapeDtypeStruct((PEER_HEADS, PEER_TOPK, rows), I32),
                   jax.ShapeDtypeStruct((PEER_HEADS, PEER_TOPK, rows), F32)],
        compiler_params=_params(("arbitrary",)),
        name="peer_route",
    )(qp, sk)


def _pack_bf16_pairs(t):
    half = t.shape[1] // 2
    bits = lambda z: lax.bitcast_convert_type(z.astype(BF16), jnp.uint16).astype(jnp.uint32)
    return lax.bitcast_convert_type((bits(t[:, :half]) << 16) | bits(t[:, half:]), I32)


def _hi_half(w):
    return pltpu.bitcast(w & jnp.int32(-65536), F32)


def _lo_half(w):
    return pltpu.bitcast(lax.shift_left(w, jnp.int32(16)), F32)


def _peer_ffn_kernel(idx_ref, gate_ref, hn_ref, h_ref, uv_ref, y_ref, buf, out_sc, sem):
    n_groups = PEER_TOK_BLOCK // PEER_UNROLL
    half = D_MODEL // 2

    n_tiles = half // LANES
    per_tile = PEER_SLOTS // (2 * n_tiles)

    def issue(tok, slot, ks):
        for k in ks:
            pltpu.make_async_copy(uv_ref.at[idx_ref[tok, k]], buf.at[slot, pl.ds(k, 1)], sem.at[slot]).start(
                priority=k % PEER_DMA_QUEUES)

    def wait(slot):
        pltpu.make_async_copy(uv_ref.at[pl.ds(0, PEER_SLOTS), 0], buf.at[slot], sem.at[slot]).wait()

    for j in range(PEER_AHEAD):
        issue(j, j % PEER_BUFS, range(PEER_SLOTS))

    def group(gi, last):
        t0 = pl.multiple_of(gi * PEER_UNROLL, PEER_UNROLL)
        gates_t = gate_ref[pl.ds(t0, PEER_UNROLL), :].T
        x_grp = hn_ref[pl.ds(t0, PEER_UNROLL), :]
        for j in range(PEER_UNROLL):
            slot = j % PEER_BUFS
            ahead = j + PEER_AHEAD
            prefetch = ahead < PEER_UNROLL or not last

            def issue_part(part):
                if prefetch:
                    issue(t0 + ahead, ahead % PEER_BUFS, range(part * per_tile, (part + 1) * per_tile))

            wait(slot)
            hacc = jnp.zeros((PEER_SLOTS, LANES), F32)
            for c in range(n_tiles):
                issue_part(c)
                lo_cols = slice(c * LANES, (c + 1) * LANES)
                hi_cols = slice(half + c * LANES, half + (c + 1) * LANES)
                w = buf[slot, :, lo_cols]
                hacc = hacc + _hi_half(w) * x_grp[j:j + 1, lo_cols] + _lo_half(w) * x_grp[j:j + 1, hi_cols]
            hcol = jnp.sum(hacc, axis=1, keepdims=True)
            act = gates_t[:, j:j + 1] * jax.nn.gelu(hcol)
            act_b = jnp.broadcast_to(act, (PEER_SLOTS, LANES))
            for c in range(n_tiles):
                issue_part(n_tiles + c)
                lo_cols = slice(c * LANES, (c + 1) * LANES)
                hi_cols = slice(half + c * LANES, half + (c + 1) * LANES)
                w = buf[slot, :, half + c * LANES:half + (c + 1) * LANES]
                out_sc[j:j + 1, lo_cols] = jnp.sum(_hi_half(w) * act_b, axis=0, keepdims=True)
                out_sc[j:j + 1, hi_cols] = jnp.sum(_lo_half(w) * act_b, axis=0, keepdims=True)
        y_ref[pl.ds(t0, PEER_UNROLL), :] = h_ref[pl.ds(t0, PEER_UNROLL), :] + out_sc[...]

    def full_group(gi, carry):
        group(gi, last=False)
        return carry

    lax.fori_loop(0, n_groups - 1, full_group, 0)
    group(n_groups - 1, last=True)


def _peer_ffn(experts_tok, gates_tok, hn2, h, uv):
    rows = h.shape[0]
    tb = PEER_TOK_BLOCK
    row = lambda w: pl.BlockSpec((tb, w), lambda i: (i, 0))
    return pl.pallas_call(
        _peer_ffn_kernel,
        grid=(rows // tb,),
        in_specs=[pl.BlockSpec((tb, PEER_SLOTS), lambda i: (i, 0), memory_space=pltpu.SMEM),
                  row(PEER_SLOTS), row(D_MODEL), row(D_MODEL), pl.BlockSpec(memory_space=pl.ANY)],
        out_specs=row(D_MODEL),
        out_shape=jax.ShapeDtypeStruct((rows, D_MODEL), F32),
        scratch_shapes=[pltpu.VMEM((PEER_BUFS, PEER_SLOTS, D_MODEL), I32), pltpu.VMEM((PEER_UNROLL, D_MODEL), F32),
                        pltpu.SemaphoreType.DMA((PEER_BUFS,))],
        compiler_params=pltpu.CompilerParams(dimension_semantics=("arbitrary",), vmem_limit_bytes=VMEM_LIMIT,
                                             disable_bounds_checks=True),
        name="peer_ffn",
    )(experts_tok, gates_tok, hn2, h, uv)


def _step(dm, x_prompt, x_sample, cache_k, cache_v, cache_kidx, state_conv, page_table, meta_tokens, rel_bias,
          g_attn, w_in, g_q, g_k, conv_dw_w, conv_dw_b, conv_ln_g, conv_ln_b, w_out, g_ffn, peer_wq, peer_subkeys,
          peer_u, peer_v):
    bsz, t, tp, db, ds = dm.batch, dm.t_prompt, dm.tp, dm.dec_batch, dm.dec_seq
    pr, sr = dm.prompt_rows, dm.sample_rows

    meta = meta_tokens.astype(x_prompt.dtype)
    seq_pad = jnp.zeros((tp - t, D_MODEL), x_prompt.dtype)
    pieces = [p for b in range(bsz) for p in (meta, x_prompt[b], seq_pad)]
    pieces += [x_sample.reshape(sr, D_MODEL), jnp.zeros((dm.rows - pr - sr, D_MODEL), x_prompt.dtype)]
    x_all = jnp.concatenate(pieces, axis=0)

    q_b, k_f, v_f, k_b, v_b, qi_b, kw, ki_b, a_all = _in_projection(x_all, g_attn[0], w_in[0], g_q[0], g_k[0])

    pb, dn, dc = _bias_tiles(rel_bias)
    far = rel_bias[N_BUCKETS - 1]
    cb = jnp.broadcast_to(far[:, None, None], (N_HEADS, 1, QB))
    o_p = _prompt_attention(dm, q_b, qi_b, kw, ki_b, k_b, v_b, pb, cb)

    s0, s1 = pr, pr + sr
    qi_s = qi_b[:, s0:s1].reshape(IDX_HEADS, db, ds, IDX_DIM).transpose(1, 0, 2, 3).reshape(db, IDX_HEADS * ds, IDX_DIM)
    w_col = kw[s0:s1, IDX_DIM:IDX_DIM + IDX_HEADS].reshape(db, ds, IDX_HEADS).transpose(0, 2, 1)
    w_col = w_col.reshape(db, IDX_HEADS * ds, 1)
    q_s = q_b[s0:s1].reshape(db, ds, N_KV_HEADS, GROUP, HEAD_DIM).transpose(0, 2, 3, 1, 4)
    q_s = q_s.reshape(db, N_HEADS * ds, HEAD_DIM)
    pad_keys = lambda z: jnp.pad(z.reshape(db, ds, -1), ((0, 0), (0, PAGE_SIZE - ds), (0, 0)))
    kv_rows = lambda z: pad_keys(z).reshape(db, PAGE_SIZE * N_KV_HEADS, HEAD_DIM)
    by_rows = lambda z: jnp.repeat(z.reshape(N_HEADS * SUBLANES, LANES), N_KV_HEADS, axis=1)
    bfar = jnp.broadcast_to(far[:, None, None], (N_HEADS, SUBLANES, LANES))
    o_s = _decode_attention(
        dm, page_table, cache_k, cache_v, cache_kidx, qi_s, w_col, jnp.swapaxes(pad_keys(ki_b[s0:s1]), 1, 2), q_s,
        kv_rows(k_b[s0:s1]), kv_rows(v_b[s0:s1]), by_rows(bfar), by_rows(dn), by_rows(dc))
    o_s = o_s.reshape(db, N_HEADS, ds, HEAD_DIM).transpose(0, 2, 1, 3).reshape(sr, ATTN_WIDTH)

    cw = _conv_weights(conv_dw_w[0], conv_dw_b[0], conv_ln_g[0], conv_ln_b[0])
    c_p = _prompt_conv(dm, a_all, cw)
    a_s = a_all[s0:s1].reshape(db, ds, CONV_CH)
    ext_s = jnp.concatenate([jnp.zeros((db, CONV_HALO - (CONV_W - 1), CONV_CH), F32), state_conv[0], a_s], axis=1)
    c_s = _sample_conv(dm, ext_s, cw).reshape(sr, CONV_CH)

    tail = jnp.zeros((dm.rows - pr - sr, ATTN_WIDTH), BF16)
    o_all = jnp.concatenate([o_p, o_s.astype(BF16), tail], axis=0)
    c_all = jnp.concatenate([c_p, c_s.astype(BF16), tail], axis=0)
    h, hn2, qp = _out_projection(o_all, c_all, x_all, w_out[0], g_ffn[0], peer_wq[0])

    experts, gates = _peer_route(qp, peer_subkeys[0])
    experts_tok = experts.reshape(PEER_SLOTS, dm.rows).T
    gates_tok = gates.reshape(PEER_SLOTS, dm.rows).T
    uv = jnp.concatenate([_pack_bf16_pairs(peer_u[0]), _pack_bf16_pairs(peer_v[0])], axis=1)[:, None, :]
    y = _peer_ffn(experts_tok, gates_tok, hn2, h, uv)

    per_seq = lambda z, lo, hi, w: jnp.stack([z[b * tp + lo:b * tp + hi, :w] for b in range(bsz)])
    y_prompt = per_seq(y, N_META, t, D_MODEL)
    y_sample = y[s0:s1].reshape(db, ds, D_MODEL)
    kv_shape = (N_KV_HEADS, HEAD_DIM)
    new_k_p = per_seq(k_f, 0, t, KV_WIDTH).reshape(1, bsz, t, *kv_shape)
    new_v_p = per_seq(v_f, 0, t, KV_WIDTH).reshape(1, bsz, t, *kv_shape)
    new_ki_p = per_seq(kw, 0, t, IDX_DIM)[None]
    new_conv_p = per_seq(a_all, t - (CONV_W - 1), t, CONV_CH)[None]
    new_k_s = k_f[s0:s1].reshape(1, db, ds, *kv_shape)
    new_v_s = v_f[s0:s1].reshape(1, db, ds, *kv_shape)
    new_ki_s = kw[s0:s1, :IDX_DIM].reshape(1, db, ds, IDX_DIM)
    new_conv_s = ext_s[:, -(CONV_W - 1):][None]
    return (y_prompt, y_sample, new_k_p, new_v_p, new_ki_p, new_conv_p, new_k_s, new_v_s, new_ki_s, new_conv_s)


def kernel(x_prompt, x_sample, cache_k, cache_v, cache_kidx, state_conv, page_table, meta_tokens, rel_bias, g_attn,
           w_in, g_q, g_k, conv_dw_w, conv_dw_b, conv_ln_g, conv_ln_b, w_out, g_ffn, peer_wq, peer_subkeys, peer_u,
           peer_v):
    assert g_attn.shape[0] == 1, "single trunk layer"
    dm = Dims(batch=x_prompt.shape[0], seq=x_prompt.shape[1], dec_batch=x_sample.shape[0],
              dec_seq=x_sample.shape[1], past_len=page_table.shape[1] * PAGE_SIZE)
    assert dm.dec_seq == SUBLANES and dm.t_prompt >= CONV_W - 1
    return _step(dm, x_prompt, x_sample, cache_k, cache_v, cache_kidx, state_conv, page_table, meta_tokens, rel_bias,
                 g_attn, w_in, g_q, g_k, conv_dw_w, conv_dw_b, conv_ln_g, conv_ln_b, w_out, g_ffn, peer_wq,
                 peer_subkeys, peer_u, peer_v)
```
